```python
import math
import jax, jax.numpy as jnp
from jax import lax
import numpy as np

D_MODEL = 1024
BATCH = 2
SEQ = 8192
DEPTH = 4
DEC_BATCH = 32
DEC_SEQ = 1
PAST_LEN = 8192
PAGE_SIZE = 128

HEAD_DIM = 64
D_MIX = D_MODEL
H_FOX = (3 * D_MIX) // (8 * HEAD_DIM)
H_SB = (3 * D_MIX) // (8 * HEAD_DIM)
D_FOX = H_FOX * HEAD_DIM
D_SB = H_SB * HEAD_DIM
D_CONV = D_MIX - D_FOX - D_SB
CONV_WIDTH = 3
N_GROUPS = 4
EXPERTS_PER_GROUP = 8
N_EXPERTS = N_GROUPS * EXPERTS_PER_GROUP
TOP_K = 2
D_EXPERT = D_MODEL // 8
Q_BLOCK = 128
EPS = 1e-6
FORGET_BIAS = 3.0
FORGET_W_SCALE = 0.1
SPLIT_SIZES = (D_FOX, D_FOX, D_FOX, H_FOX, D_CONV, D_CONV, D_CONV, D_SB, D_SB, D_SB)
N_IN = 3 * D_FOX + H_FOX + 3 * D_CONV + 3 * D_SB

kernel_name = 'hybrid_fox_conv_stickbreak_hmoe_step'


def rmsnorm(x, g):
    x32 = x.astype(jnp.float32)
    y = x32 * lax.rsqrt(jnp.mean(x32 * x32, axis=-1, keepdims=True) + EPS)
    return y.astype(x.dtype) * g


def split_columns(p):
    points, acc = [], 0
    for s in SPLIT_SIZES[:-1]:
        acc += s
        points.append(acc)
    return jnp.split(p, points, axis=-1)


def fox_attend(q, k, v, fq, fk, q_pos):
    s = jnp.einsum('bqhd,bkhd->bhqk', q, k, preferred_element_type=jnp.float32) * (HEAD_DIM ** -0.5)
    s = s + jnp.swapaxes(fq, 1, 2)[:, :, :, None] - jnp.swapaxes(fk, 1, 2)[:, :, None, :]
    causal = jnp.arange(k.shape[1])[None, :] <= q_pos[:, None]
    p = jax.nn.softmax(jnp.where(causal, s, -jnp.inf), axis=-1)
    return jnp.einsum('bhqk,bkhd->bqhd', p.astype(v.dtype), v)


def stick_breaking_attend(q, k, v, q_pos):
    z = jnp.einsum('bqhd,bkhd->bhqk', q, k, preferred_element_type=jnp.float32) * (HEAD_DIM ** -0.5)
    strict = jnp.arange(k.shape[1])[None, :] < q_pos[:, None]
    log_keep = jnp.where(strict, jax.nn.log_sigmoid(-z), 0.0)
    after = lax.cumsum(log_keep, axis=3, reverse=True) - log_keep
    a = jnp.where(strict, jnp.exp(jax.nn.log_sigmoid(z) + after), 0.0)
    return jnp.einsum('bhqk,bkhd->bqhd', a.astype(v.dtype), v)


def sweep_queries(block_fn, t):
    if t % Q_BLOCK != 0:
        return block_fn(0, t)
    starts = jnp.arange(t // Q_BLOCK) * Q_BLOCK
    out = lax.map(lambda st: block_fn(st, Q_BLOCK), starts)
    out = jnp.moveaxis(out, 0, 1)
    return out.reshape(out.shape[0], t, out.shape[3], out.shape[4])


def short_conv(u, prev, w):
    t = u.shape[1]
    full = jnp.concatenate([prev.astype(u.dtype), u], axis=1)
    y = full[:, 0:t] * w[0]
    for i in range(1, CONV_WIDTH):
        y = y + full[:, i:i + t] * w[i]
    return y, full[:, full.shape[1] - (CONV_WIDTH - 1):]


def token_mixers(xn, past, w_in_l, b_f_l, conv_w_l, g_out_l, w_out_l):
    b, t, _ = xn.shape
    (q_f, k_f, v_f, f_logit, gate_b, gate_c, h_c, q_s, k_s, v_s) = split_columns(xn @ w_in_l)
    q_f, k_f, v_f, q_s, k_s, v_s = [a.reshape(b, t, -1, HEAD_DIM) for a in (q_f, k_f, v_f, q_s, k_s, v_s)]
    logf = jax.nn.log_sigmoid(f_logit.astype(jnp.float32) + b_f_l.astype(jnp.float32))
    u = gate_c * h_c
    if past is None:
        kf_all, vf_all, lf_all, ks_all, vs_all = k_f, v_f, logf, k_s, v_s
        conv_prev = jnp.zeros((b, CONV_WIDTH - 1, D_CONV), u.dtype)
        pos0 = 0
    else:
        kf_p, vf_p, lf_p, ks_p, vs_p, conv_prev = past
        pos0 = kf_p.shape[1]
        kf_all = jnp.concatenate([kf_p, k_f.astype(kf_p.dtype)], axis=1)
        vf_all = jnp.concatenate([vf_p, v_f.astype(vf_p.dtype)], axis=1)
        lf_all = jnp.concatenate([lf_p.astype(jnp.float32), logf], axis=1)
        ks_all = jnp.concatenate([ks_p, k_s.astype(ks_p.dtype)], axis=1)
        vs_all = jnp.concatenate([vs_p, v_s.astype(vs_p.dtype)], axis=1)
    F = jnp.cumsum(lf_all, axis=1)
    Fq = F[:, pos0:]

    def fox_blk(start, tq):
        qb = lax.dynamic_slice_in_dim(q_f, start, tq, axis=1)
        fqb = lax.dynamic_slice_in_dim(Fq, start, tq, axis=1)
        return fox_attend(qb, kf_all, vf_all, fqb, F, pos0 + start + jnp.arange(tq))

    def sb_blk(start, tq):
        qb = lax.dynamic_slice_in_dim(q_s, start, tq, axis=1)
        return stick_breaking_attend(qb, ks_all, vs_all, pos0 + start + jnp.arange(tq))

    o_f = sweep_queries(fox_blk, t).reshape(b, t, D_FOX)
    o_s = sweep_queries(sb_blk, t).reshape(b, t, D_SB)
    y_c, conv_new = short_conv(u, conv_prev, conv_w_l)
    o_c = gate_b * y_c
    o = jnp.concatenate([
        rmsnorm(o_f, g_out_l[:D_FOX]),
        rmsnorm(o_c, g_out_l[D_FOX:D_FOX + D_CONV]),
        rmsnorm(o_s, g_out_l[D_FOX + D_CONV:])], axis=-1)
    return o @ w_out_l, (k_f, v_f, logf, k_s, v_s, conv_new)


def hier_moe(xn, w_rg, b_rg, w_re, b_re, w_gate, w_up, w_down):
    b, t, d = xn.shape
    x2 = xn.reshape(b * t, d)
    g_logits = (x2 @ w_rg).astype(jnp.float32) + b_rg
    g_prob = jax.nn.softmax(g_logits, axis=-1)
    _, g_idx = lax.top_k(g_logits, 1)
    p_group = jnp.take_along_axis(g_prob, g_idx, axis=-1)
    e_logits = ((x2 @ w_re).astype(jnp.float32) + b_re).reshape(b * t, N_GROUPS, EXPERTS_PER_GROUP)
    e_in = jnp.take_along_axis(e_logits, g_idx[:, :, None], axis=1)[:, 0]
    e_top, e_idx = lax.top_k(e_in, TOP_K)
    w_sel = jax.nn.softmax(e_top, axis=-1) * p_group
    expert_id = g_idx * EXPERTS_PER_GROUP + e_idx
    gate = jnp.sum(jax.nn.one_hot(expert_id, N_EXPERTS, dtype=jnp.float32) * w_sel[:, :, None], axis=1)
    h = jax.nn.silu(jnp.einsum('td,edf->tef', x2, w_gate)) * jnp.einsum('td,edf->tef', x2, w_up)
    y = jnp.einsum('tef,efd->td', h * gate[:, :, None].astype(h.dtype), w_down)
    return y.reshape(b, t, d)


def gather_pages(cache, layer, page_table):
    g = cache[layer, page_table]
    return g.reshape((g.shape[0], g.shape[1] * g.shape[2]) + g.shape[3:])


def setup_inputs(seed: int = 0) -> dict:
    key = jax.random.key(seed)
    ks = jax.random.split(key, 32)
    f32 = jnp.float32
    n_pages = PAST_LEN // PAGE_SIZE
    n_pool = (5 * DEC_BATCH * n_pages + 3) // 4

    def nrm(k, shape, scale=1.0):
        return jax.random.normal(k, shape, f32) * scale

    x_prompt = nrm(ks[0], (BATCH, SEQ, D_MODEL))
    x_sample = nrm(ks[1], (DEC_BATCH, DEC_SEQ, D_MODEL))
    cache_k_fox = nrm(ks[2], (DEPTH, n_pool, PAGE_SIZE, H_FOX, HEAD_DIM))
    cache_v_fox = nrm(ks[3], (DEPTH, n_pool, PAGE_SIZE, H_FOX, HEAD_DIM))
    cache_logf_fox = jax.nn.log_sigmoid(FORGET_BIAS + nrm(ks[4], (DEPTH, n_pool, PAGE_SIZE, H_FOX), 0.5))
    cache_k_sb = nrm(ks[5], (DEPTH, n_pool, PAGE_SIZE, H_SB, HEAD_DIM))
    cache_v_sb = nrm(ks[6], (DEPTH, n_pool, PAGE_SIZE, H_SB, HEAD_DIM))
    state_conv = nrm(ks[7], (DEPTH, DEC_BATCH, CONV_WIDTH - 1, D_CONV))
    perm = jax.random.permutation(ks[8], n_pool)
    page_table = perm[:DEC_BATCH * n_pages].reshape(DEC_BATCH, n_pages).astype(jnp.int32)

    g_mix_norm = 1.0 + nrm(ks[9], (DEPTH, D_MODEL), 0.02)
    w_in = nrm(ks[10], (DEPTH, D_MODEL, N_IN), D_MODEL ** -0.5)
    f0 = 3 * D_FOX
    w_in = w_in.at[:, :, f0:f0 + H_FOX].multiply(FORGET_W_SCALE)
    b_f = FORGET_BIAS + nrm(ks[11], (DEPTH, H_FOX), 0.1)
    conv_w = nrm(ks[12], (DEPTH, CONV_WIDTH, D_CONV), CONV_WIDTH ** -0.5)
    g_mix_out = 1.0 + nrm(ks[13], (DEPTH, D_MIX), 0.02)
    w_out = nrm(ks[14], (DEPTH, D_MIX, D_MODEL), D_MIX ** -0.5)
    g_ffn_norm = 1.0 + nrm(ks[15], (DEPTH, D_MODEL), 0.02)
    w_router_g = nrm(ks[16], (DEPTH, D_MODEL, N_GROUPS), D_MODEL ** -0.5)
    b_router_g = nrm(ks[17], (DEPTH, N_GROUPS), 0.01)
    w_router_e = nrm(ks[18], (DEPTH, D_MODEL, N_EXPERTS), D_MODEL ** -0.5)
    b_router_e = nrm(ks[19], (DEPTH, N_EXPERTS), 0.01)
    w_gate = nrm(ks[20], (DEPTH, N_EXPERTS, D_MODEL, D_EXPERT), D_MODEL ** -0.5)
    w_up = nrm(ks[21], (DEPTH, N_EXPERTS, D_MODEL, D_EXPERT), D_MODEL ** -0.5)
    w_down = nrm(ks[22], (DEPTH, N_EXPERTS, D_EXPERT, D_MODEL), D_EXPERT ** -0.5)
    g_final = 1.0 + nrm(ks[23], (D_MODEL,), 0.02)
    return {'x_prompt': x_prompt, 'x_sample': x_sample,
            'cache_k_fox': cache_k_fox, 'cache_v_fox': cache_v_fox, 'cache_logf_fox': cache_logf_fox,
            'cache_k_sb': cache_k_sb, 'cache_v_sb': cache_v_sb, 'state_conv': state_conv,
            'page_table': page_table,
            'g_mix_norm': g_mix_norm, 'w_in': w_in, 'b_f': b_f, 'conv_w': conv_w,
            'g_mix_out': g_mix_out, 'w_out': w_out, 'g_ffn_norm': g_ffn_norm,
            'w_router_g': w_router_g, 'b_router_g': b_router_g,
            'w_router_e': w_router_e, 'b_router_e': b_router_e,
            'w_gate': w_gate, 'w_up': w_up, 'w_down': w_down, 'g_final': g_final}


def reference(x_prompt, x_sample, cache_k_fox, cache_v_fox, cache_logf_fox, cache_k_sb, cache_v_sb,
              state_conv, page_table, g_mix_norm, w_in, b_f, conv_w, g_mix_out, w_out, g_ffn_norm,
              w_router_g, b_router_g, w_router_e, b_router_e, w_gate, w_up, w_down, g_final):
    xp, xs = x_prompt, x_sample
    st_p = [[], [], [], [], [], []]
    st_s = [[], [], [], [], [], []]
    for l in range(DEPTH):
        mix_w = (w_in[l], b_f[l], conv_w[l], g_mix_out[l], w_out[l])
        moe_w = (w_router_g[l], b_router_g[l], w_router_e[l], b_router_e[l], w_gate[l], w_up[l], w_down[l])
        m_p, new_p = token_mixers(rmsnorm(xp, g_mix_norm[l]), None, *mix_w)
        xp = xp + m_p
        xp = xp + hier_moe(rmsnorm(xp, g_ffn_norm[l]), *moe_w)
        past = (gather_pages(cache_k_fox, l, page_table), gather_pages(cache_v_fox, l, page_table),
                gather_pages(cache_logf_fox, l, page_table), gather_pages(cache_k_sb, l, page_table),
                gather_pages(cache_v_sb, l, page_table), state_conv[l])
        m_s, new_s = token_mixers(rmsnorm(xs, g_mix_norm[l]), past, *mix_w)
        xs = xs + m_s
        xs = xs + hier_moe(rmsnorm(xs, g_ffn_norm[l]), *moe_w)
        for acc, a in zip(st_p, new_p):
            acc.append(a)
        for acc, a in zip(st_s, new_s):
            acc.append(a)
    y_prompt = rmsnorm(xp, g_final)
    y_sample = rmsnorm(xs, g_final)
    new_k_fox_p, new_v_fox_p, new_logf_fox_p, new_k_sb_p, new_v_sb_p, new_conv_p = [jnp.stack(a) for a in st_p]
    new_k_fox_s, new_v_fox_s, new_logf_fox_s, new_k_sb_s, new_v_sb_s, new_conv_s = [jnp.stack(a) for a in st_s]
    return (y_prompt, y_sample,
            new_k_fox_p, new_v_fox_p, new_logf_fox_p, new_k_sb_p, new_v_sb_p, new_conv_p,
            new_k_fox_s, new_v_fox_s, new_logf_fox_s, new_k_sb_s, new_v_sb_s, new_conv_s)
```

```python
import functools

import jax
import jax.numpy as jnp
from jax import lax
from jax.experimental import pallas as pl
from jax.experimental.pallas import tpu as pltpu

D_MODEL = 1024
HEAD_DIM = 64
N_HEADS = 6
D_ATT = N_HEADS * HEAD_DIM
D_CONV = 256
CONV_WIDTH = 3
N_GROUPS = 4
EXPERTS_PER_GROUP = 8
N_EXPERTS = N_GROUPS * EXPERTS_PER_GROUP
D_EXPERT = 128
D_GROUP = EXPERTS_PER_GROUP * D_EXPERT
PAGE_SIZE = 128
EPS = 1e-6
SCALE = HEAD_DIM ** -0.5
NEG = -1e30

LANES = 128
HEAD_ROWS = 16
VMEM_LIMIT = 56 * 1024 * 1024

C_QF, C_KF, C_VF = 0, 384, 768
C_GB, C_GC, C_HC = 1152, 1408, 1664
C_QS, C_KS, C_VS = 1920, 2304, 2688
C_FL = 3072
N_IN_PAD = 3200

F32 = jnp.float32
BF16 = jnp.bfloat16


def _rms(x):
    return x * lax.rsqrt(jnp.sum(x * x, axis=-1, keepdims=True) * (1.0 / x.shape[-1]) + EPS)


def _softplus(z):
    return jnp.maximum(z, 0.0) + jnp.log1p(jnp.exp(-jnp.abs(z)))


def _split3(x):
    hi = x.astype(BF16)
    r = x - hi.astype(F32)
    mid = r.astype(BF16)
    lo = (r - mid.astype(F32)).astype(BF16)
    return hi, mid, lo


def _dot(a, b):
    return jnp.dot(a, b, preferred_element_type=F32)


def _dot_nt(a, b):
    return lax.dot_general(a, b, (((1,), (1,)), ((), ())), preferred_element_type=F32)


def _params(*sem):
    return pltpu.CompilerParams(dimension_semantics=sem, vmem_limit_bytes=VMEM_LIMIT)


def _project(x_ref, gn_ref, w_ref):
    xb = (_rms(x_ref[...]) * gn_ref[...]).astype(BF16)

    def proj(a, b):
        return _dot(xb, w_ref[:, a:b])
    return proj


def _store_qkv(proj, c_q, q_ref, k32_ref, v32_ref, k16_ref=None, v16_ref=None):
    qkv = proj(c_q, c_q + 3 * D_ATT)
    k = qkv[:, D_ATT:2 * D_ATT]
    v = qkv[:, 2 * D_ATT:]
    k32_ref[...] = k
    v32_ref[...] = v
    if k16_ref is None:
        q_ref[...] = qkv[:, :D_ATT] * SCALE
    else:
        q_ref[...] = (qkv[:, :D_ATT] * SCALE).astype(BF16)
        k16_ref[...] = k.astype(BF16)
        v16_ref[...] = v.astype(BF16)


def _log_forget(proj, bf_ref):
    fl = proj(C_FL, N_IN_PAD) + bf_ref[...]
    lane = lax.broadcasted_iota(jnp.int32, fl.shape, 1)
    return jnp.where(lane < N_HEADS, -_softplus(-fl), 0.0)


def _inproj_prompt_kernel(x_ref, gn_ref, w_ref, bf_ref, cw_ref, gc_ref, tri_ref,
                          qf_ref, kf32_ref, vf32_ref, kf16_ref, vf16_ref,
                          qs_ref, ks32_ref, vs32_ref, ks16_ref, vs16_ref,
                          logf_ref, fcum_ref, oc_ref, utail_ref,
                          ubuf, fcarry, *, tm, tiles_per_seq):
    i = pl.program_id(0)
    proj = _project(x_ref, gn_ref, w_ref)
    _store_qkv(proj, C_QF, qf_ref, kf32_ref, vf32_ref, kf16_ref, vf16_ref)
    _store_qkv(proj, C_QS, qs_ref, ks32_ref, vs32_ref, ks16_ref, vs16_ref)

    @pl.when(i % tiles_per_seq == 0)
    def _():
        ubuf[0:8, :] = jnp.zeros((8, D_CONV), F32)
        fcarry[...] = jnp.zeros((1, LANES), F32)

    logf = _log_forget(proj, bf_ref)
    logf_ref[...] = logf
    tri = tri_ref[...]
    hi, mid, lo = _split3(logf)
    fcum = _dot(tri, hi) + _dot(tri, mid) + _dot(tri, lo) + fcarry[...]
    fcum_ref[...] = fcum
    fcarry[...] = fcum[tm - 1:tm, :]

    gch = proj(C_GB, C_GB + 3 * D_CONV)
    u = gch[:, D_CONV:2 * D_CONV] * gch[:, 2 * D_CONV:]
    ubuf[8:tm + 8, :] = u
    y = (ubuf[6:tm + 6, :] * cw_ref[0:1, :] + ubuf[7:tm + 7, :] * cw_ref[1:2, :]
         + u * cw_ref[2:3, :])
    tail = u[tm - 8:tm, :]
    ubuf[0:8, :] = tail
    utail_ref[...] = tail
    oc_ref[...] = (_rms(gch[:, :D_CONV] * y) * gc_ref[...]).astype(BF16)


def _inproj_sample_kernel(x_ref, gn_ref, w_ref, bf_ref, cw_ref, gc_ref, p0_ref, p1_ref,
                          qf_ref, kf32_ref, vf32_ref, qs_ref, ks32_ref, vs32_ref,
                          logf_ref, oc_ref, u_ref):
    proj = _project(x_ref, gn_ref, w_ref)
    _store_qkv(proj, C_QF, qf_ref, kf32_ref, vf32_ref)
    _store_qkv(proj, C_QS, qs_ref, ks32_ref, vs32_ref)
    logf_ref[...] = _log_forget(proj, bf_ref)
    gch = proj(C_GB, C_GB + 3 * D_CONV)
    u = gch[:, D_CONV:2 * D_CONV] * gch[:, 2 * D_CONV:]
    u_ref[...] = u
    y = p0_ref[...] * cw_ref[0:1, :] + p1_ref[...] * cw_ref[1:2, :] + u * cw_ref[2:3, :]
    oc_ref[...] = (_rms(gch[:, :D_CONV] * y) * gc_ref[...]).astype(BF16)


def _full(shape):
    return pl.BlockSpec(shape, lambda *_: (0,) * len(shape))


def _inproj_prompt(x, gn, w, bf, cw, gc, tri, *, seq, tm):
    t = x.shape[0]
    n_seq = t // seq
    tiles_per_seq = seq // tm
    row = lambda width: pl.BlockSpec((tm, width), lambda i: (i, 0))
    att = lambda dt: jax.ShapeDtypeStruct((t, D_ATT), dt)
    out_shape = ([att(BF16), att(F32), att(F32), att(BF16), att(BF16)] * 2
                 + [jax.ShapeDtypeStruct((t, LANES), F32), jax.ShapeDtypeStruct((t, LANES), F32),
                    jax.ShapeDtypeStruct((t, D_CONV), BF16),
                    jax.ShapeDtypeStruct((n_seq * 8, D_CONV), F32)])
    out_specs = ([row(D_ATT)] * 10 + [row(LANES), row(LANES), row(D_CONV),
                 pl.BlockSpec((8, D_CONV), lambda i: (i // tiles_per_seq, 0))])
    return pl.pallas_call(
        functools.partial(_inproj_prompt_kernel, tm=tm, tiles_per_seq=tiles_per_seq),
        grid=(t // tm,),
        in_specs=[row(D_MODEL), _full((1, D_MODEL)), _full((D_MODEL, N_IN_PAD)), _full((1, LANES)),
                  _full((8, D_CONV)), _full((1, D_CONV)), _full((tm, tm))],
        out_specs=out_specs, out_shape=out_shape,
        scratch_shapes=[pltpu.VMEM((tm + 8, D_CONV), F32), pltpu.VMEM((1, LANES), F32)],
        compiler_params=_params("arbitrary"), name="inproj_prompt",
    )(x, gn, w, bf, cw, gc, tri)


def _inproj_sample(x, gn, w, bf, cw, gc, p0, p1):
    t = x.shape[0]
    att = jax.ShapeDtypeStruct((t, D_ATT), F32)
    out_shape = [att] * 6 + [jax.ShapeDtypeStruct((t, LANES), F32),
                             jax.ShapeDtypeStruct((t, D_CONV), BF16),
                             jax.ShapeDtypeStruct((t, D_CONV), F32)]
    return pl.pallas_call(
        _inproj_sample_kernel, out_shape=out_shape,
        compiler_params=pltpu.CompilerParams(vmem_limit_bytes=VMEM_LIMIT), name="inproj_sample",
    )(x, gn, w, bf, cw, gc, p0, p1)


def _split_heads(q_ref, q_s):
    q = q_ref[...]
    lo = lax.broadcasted_iota(jnp.int32, q.shape, 1) < HEAD_DIM
    zero = jnp.zeros_like(q)
    q_s[0] = jnp.where(lo, q, zero)
    q_s[1] = jnp.where(lo, zero, q)


def _fox_prompt_kernel(q_ref, k_ref, v_ref, fk_ref, o_ref, q_s, m_s, l_s, acc_s, *, tq, tk):
    qi = pl.program_id(2)
    _split_heads(q_ref, q_s)
    m_s[...] = jnp.full(m_s.shape, NEG, F32)
    l_s[...] = jnp.zeros(l_s.shape, F32)
    acc_s[...] = jnp.zeros(acc_s.shape, F32)

    def block(j, masked):
        start = pl.multiple_of(j * tk, tk)
        k = k_ref[pl.ds(start, tk), :]
        v = v_ref[pl.ds(start, tk), :]
        if masked:
            row = lax.broadcasted_iota(jnp.int32, (tq, tk), 0) + qi * tq
            col = lax.broadcasted_iota(jnp.int32, (tq, tk), 1) + start
            causal = col <= row
        for h in range(2):
            s = _dot_nt(q_s[h], k) - fk_ref[h:h + 1, pl.ds(start, tk)]
            if masked:
                s = jnp.where(causal, s, NEG)
            m_old = m_s[h]
            m_new = jnp.maximum(m_old, jnp.max(s, axis=-1, keepdims=True))
            alpha = jnp.exp(m_old - m_new)
            p = jnp.exp(s - m_new)
            l_s[h] = alpha * l_s[h] + jnp.sum(p, axis=-1, keepdims=True)
            acc_s[h] = alpha * acc_s[h] + _dot(p.astype(BF16), v)
            m_s[h] = m_new

    n_diag = tq // tk
    lax.fori_loop(0, qi * n_diag, lambda j, c: (block(j, False), c)[1], 0)
    for d in range(n_diag):
        block(qi * n_diag + d, True)

    lo = lax.broadcasted_iota(jnp.int32, (tq, LANES), 1) < HEAD_DIM
    o_ref[...] = jnp.where(lo, acc_s[0] / l_s[0], acc_s[1] / l_s[1])


def _sb_prompt_kernel(q_ref, k_ref, v_ref, su_ref, o_ref, q_s, r_s, acc_s, *, tq, tk):
    qi = pl.program_id(2)
    _split_heads(q_ref, q_s)
    r_s[...] = jnp.zeros(r_s.shape, F32)
    acc_s[...] = jnp.zeros(acc_s.shape, F32)

    def block(j, masked):
        start = pl.multiple_of(j * tk, tk)
        k = k_ref[pl.ds(start, tk), :]
        v = v_ref[pl.ds(start, tk), :]
        su = su_ref[...]
        if masked:
            row = lax.broadcasted_iota(jnp.int32, (tq, tk), 0) + qi * tq
            col = lax.broadcasted_iota(jnp.int32, (tq, tk), 1) + start
            strict = col < row
        for h in range(2):
            z = _dot_nt(q_s[h], k)
            lk = -_softplus(z)
            if masked:
                lk = jnp.where(strict, lk, 0.0)
            hi, mid, lo = _split3(lk)
            after = _dot(hi, su) + _dot(mid, su) + _dot(lo, su) + r_s[h]
            a = jnp.exp(z + lk + after)
            if masked:
                a = jnp.where(strict, a, 0.0)
            acc_s[h] = acc_s[h] + _dot(a.astype(BF16), v)
            r_s[h] = r_s[h] + jnp.sum(lk, axis=-1, keepdims=True)

    n_diag = tq // tk
    for d in reversed(range(n_diag)):
        block(qi * n_diag + d, True)
    n_full = qi * n_diag
    lax.fori_loop(0, n_full, lambda i, c: (block(n_full - 1 - i, False), c)[1], 0)

    lo = lax.broadcasted_iota(jnp.int32, (tq, LANES), 1) < HEAD_DIM
    o_ref[...] = jnp.where(lo, acc_s[0], acc_s[1])


def _attn_prompt_specs(seq, n_seq, tq):
    nq = seq // tq
    grid = (n_seq, D_ATT // LANES, nq)
    q_spec = pl.BlockSpec((tq, LANES), lambda b, p, i: (b * nq + i, p))
    kv_spec = pl.BlockSpec((seq, LANES), lambda b, p, i: (b, p))
    return grid, q_spec, kv_spec


def _fox_prompt(q, k, v, fk, *, seq, tq, tk):
    t = q.shape[0]
    grid, q_spec, kv_spec = _attn_prompt_specs(seq, t // seq, tq)
    return pl.pallas_call(
        functools.partial(_fox_prompt_kernel, tq=tq, tk=tk), grid=grid,
        in_specs=[q_spec, kv_spec, kv_spec,
                  pl.BlockSpec((None, None, 8, seq), lambda b, p, i: (b, p, 0, 0))],
        out_specs=q_spec, out_shape=jax.ShapeDtypeStruct((t, D_ATT), F32),
        scratch_shapes=[pltpu.VMEM((2, tq, LANES), BF16), pltpu.VMEM((2, tq, 1), F32),
                        pltpu.VMEM((2, tq, 1), F32), pltpu.VMEM((2, tq, LANES), F32)],
        compiler_params=_params("arbitrary", "arbitrary", "arbitrary"), name="fox_prompt",
    )(q, k, v, fk)


def _sb_prompt(q, k, v, su, *, seq, tq, tk):
    t = q.shape[0]
    grid, q_spec, kv_spec = _attn_prompt_specs(seq, t // seq, tq)
    return pl.pallas_call(
        functools.partial(_sb_prompt_kernel, tq=tq, tk=tk), grid=grid,
        in_specs=[q_spec, kv_spec, kv_spec, pl.BlockSpec((tk, tk), lambda b, p, i: (0, 0))],
        out_specs=q_spec, out_shape=jax.ShapeDtypeStruct((t, D_ATT), F32),
        scratch_shapes=[pltpu.VMEM((2, tq, LANES), BF16), pltpu.VMEM((2, tq, 1), F32),
                        pltpu.VMEM((2, tq, LANES), F32)],
        compiler_params=_params("arbitrary", "arbitrary", "arbitrary"), name="sb_prompt",
    )(q, k, v, su)


def _pick_rows(parts):
    rows = lax.broadcasted_iota(jnp.int32, parts[0].shape, 0)
    out = jnp.zeros_like(parts[0])
    for h, part in enumerate(parts):
        out = jnp.where(rows == h, part, out)
    return out


def _suffix_sum(x, su):
    hi, mid, lo = _split3(x)
    return _dot(hi, su) + _dot(mid, su) + _dot(lo, su)


def _decode_attn_kernel(pt_ref, qf_ref, knf_ref, vnf_ref, lfn_ref, qs_ref, su_ref,
                        kf_ref, vf_ref, lf_ref, ks_ref, vs_ref,
                        of_ref, os_ref,
                        m_s, l_s, g_s, accf_s, r_s, accs_s):
    del pt_ref
    p = pl.program_id(1)
    n_pages = pl.num_programs(1)

    @pl.when(p == 0)
    def _():
        m_s[...] = jnp.full(m_s.shape, NEG, F32)
        l_s[...] = jnp.zeros(l_s.shape, F32)
        g_s[...] = jnp.zeros(g_s.shape, F32)
        r_s[...] = jnp.zeros(r_s.shape, F32)
        accf_s[...] = jnp.zeros(accf_s.shape, F32)
        accs_s[...] = jnp.zeros(accs_s.shape, F32)

    su = su_ref[...]
    qf = qf_ref[...]
    qs = qs_ref[...]

    def scores(q, k_ref):
        qb = q.astype(BF16)
        return _pick_rows([_dot_nt(qb, k_ref[:, h, :].astype(BF16)) for h in range(N_HEADS)])

    lf = jnp.concatenate([lf_ref[...], jnp.zeros((HEAD_ROWS - 8, PAGE_SIZE), F32)], axis=0)
    s = scores(qf, kf_ref) + _suffix_sum(lf, su) + g_s[...] + lfn_ref[...]
    m_old = m_s[...]
    m_new = jnp.maximum(m_old, jnp.max(s, axis=-1, keepdims=True))
    alpha = jnp.exp(m_old - m_new)
    pr = jnp.exp(s - m_new)
    l_s[...] = alpha * l_s[...] + jnp.sum(pr, axis=-1, keepdims=True)
    m_s[...] = m_new
    g_s[...] = g_s[...] + jnp.sum(lf, axis=-1, keepdims=True)
    prb = pr.astype(BF16)
    for h in range(N_HEADS):
        accf_s[h] = alpha * accf_s[h] + _dot(prb, vf_ref[:, h, :].astype(BF16))

    z = scores(qs, ks_ref)
    lk = -_softplus(z)
    a = jnp.exp(z + lk + _suffix_sum(lk, su) + r_s[...])
    r_s[...] = r_s[...] + jnp.sum(lk, axis=-1, keepdims=True)
    ab = a.astype(BF16)
    for h in range(N_HEADS):
        accs_s[h] = accs_s[h] + _dot(ab, vs_ref[:, h, :].astype(BF16))

    @pl.when(p == n_pages - 1)
    def _():
        s_self = jnp.sum(qf * knf_ref[...], axis=-1, keepdims=True)
        m_old = m_s[...]
        m_new = jnp.maximum(m_old, s_self)
        alpha = jnp.exp(m_old - m_new)
        p_self = jnp.exp(s_self - m_new)
        inv_l = 1.0 / (alpha * l_s[...] + p_self)
        own = p_self * vnf_ref[...]
        of_ref[...] = _pick_rows([alpha * accf_s[h] + own for h in range(N_HEADS)]) * inv_l
        os_ref[...] = _pick_rows([accs_s[h] for h in range(N_HEADS)])


def _decode_attn(layer, page_table, qf, knf, vnf, lfn, qs, su,
                 cache_kf, cache_vf, cache_lf_t, cache_ks, cache_vs):
    n_b, n_pages = page_table.shape
    per_b = lambda shape: pl.BlockSpec((None,) + shape, lambda b, p, pt: (b,) + (0,) * len(shape))

    def page(shape):
        return pl.BlockSpec((None, None) + shape,
                            lambda b, p, pt: (layer, pt[b, n_pages - 1 - p]) + (0,) * len(shape))

    head_blk = per_b((HEAD_ROWS, HEAD_DIM))
    kv_page = page((PAGE_SIZE, N_HEADS, HEAD_DIM))
    out = jax.ShapeDtypeStruct((n_b, HEAD_ROWS, HEAD_DIM), F32)
    col = pltpu.VMEM((HEAD_ROWS, 1), F32)
    acc = pltpu.VMEM((N_HEADS, HEAD_ROWS, HEAD_DIM), F32)
    return pl.pallas_call(
        _decode_attn_kernel,
        grid_spec=pltpu.PrefetchScalarGridSpec(
            num_scalar_prefetch=1, grid=(n_b, n_pages),
            in_specs=[head_blk, head_blk, head_blk, per_b((HEAD_ROWS, 1)), head_blk,
                      pl.BlockSpec((PAGE_SIZE, PAGE_SIZE), lambda b, p, pt: (0, 0)),
                      kv_page, kv_page, page((8, PAGE_SIZE)), kv_page, kv_page],
            out_specs=[head_blk, head_blk],
            scratch_shapes=[col, col, col, acc, col, acc]),
        out_shape=[out, out],
        compiler_params=_params("arbitrary", "arbitrary"), name="decode_attn",
    )(page_table, qf, knf, vnf, lfn, qs, su, cache_kf, cache_vf, cache_lf_t, cache_ks, cache_vs)


def _outproj_kernel(of_ref, oc_ref, os_ref, x_ref, gf_ref, gs_ref, w_ref, o_ref):
    nf = (_rms(of_ref[...]) * gf_ref[...]).astype(BF16)
    ns = (_rms(os_ref[...]) * gs_ref[...]).astype(BF16)
    y = (_dot(nf, w_ref[0:D_ATT, :]) + _dot(oc_ref[...], w_ref[D_ATT:D_ATT + D_CONV, :])
         + _dot(ns, w_ref[D_ATT + D_CONV:, :]))
    o_ref[...] = x_ref[...] + y


def _outproj(of, oc, os_, x, gf, gs, w, *, tm):
    t = x.shape[0]
    row = lambda width: pl.BlockSpec((tm, width), lambda i: (i, 0))
    return pl.pallas_call(
        _outproj_kernel, grid=(t // tm,),
        in_specs=[row(D_ATT), row(D_CONV), row(D_ATT), row(D_MODEL), _full((1, D_ATT)),
                  _full((1, D_ATT)), _full((D_MODEL, D_MODEL))],
        out_specs=row(D_MODEL), out_shape=jax.ShapeDtypeStruct((t, D_MODEL), F32),
        compiler_params=_params("arbitrary"), name="outproj",
    )(of, oc, os_, x, gf, gs, w)


def _route(logits):
    lane_i = lax.broadcasted_iota(jnp.int32, logits.shape, 1)
    lane = lane_i.astype(F32)
    lane_group = jnp.right_shift(lane_i, EXPERTS_PER_GROUP.bit_length() - 1).astype(F32)
    big = float(LANES)
    is_group = (lane_i >= N_EXPERTS) & (lane_i < N_EXPERTS + N_GROUPS)
    gl = jnp.where(is_group, logits, NEG)
    g_max = jnp.max(gl, axis=-1, keepdims=True)
    g_idx = jnp.min(jnp.where(gl == g_max, lane, big), axis=-1, keepdims=True) - N_EXPERTS
    p_group = 1.0 / jnp.sum(jnp.exp(gl - g_max), axis=-1, keepdims=True)
    in_group = (lane_i < N_EXPERTS) & (lane_group == g_idx)
    el = jnp.where(in_group, logits, NEG)
    m1 = jnp.max(el, axis=-1, keepdims=True)
    i1 = jnp.min(jnp.where(el == m1, lane, big), axis=-1, keepdims=True)
    el2 = jnp.where(lane == i1, NEG, el)
    m2 = jnp.max(el2, axis=-1, keepdims=True)
    i2 = jnp.min(jnp.where(el2 == m2, lane, big), axis=-1, keepdims=True)
    e21 = jnp.exp(m2 - m1)
    w1 = p_group / (1.0 + e21)
    return jnp.where(lane == i1, w1, 0.0) + jnp.where(lane == i2, w1 * e21, 0.0)


def _moe_kernel(x_ref, gn_ref, wr_ref, br_ref, wg_ref, wu_ref, wd_ref, gfin_ref, o_ref,
                xn_s, gate_s, *, final):
    g = pl.program_id(1)

    @pl.when(g == 0)
    def _():
        x = x_ref[...]
        xn = _rms(x) * gn_ref[...]
        xn_s[...] = xn.astype(BF16)
        logits = jnp.dot(xn, wr_ref[...], preferred_element_type=F32,
                         precision=lax.Precision.HIGHEST) + br_ref[...]
        gate = _route(logits)
        for gi in range(N_GROUPS):
            shift = (LANES - gi * EXPERTS_PER_GROUP) % LANES
            gate_s[gi] = pltpu.roll(gate, shift, 1) if shift else gate
        o_ref[...] = x

    xb = xn_s[...]
    hg = _dot(xb, wg_ref[...])
    hu = _dot(xb, wu_ref[...])
    h = hg * (1.0 / (1.0 + jnp.exp(-hg))) * hu
    gate = gate_s[g]
    hgated = jnp.concatenate(
        [h[:, e * D_EXPERT:(e + 1) * D_EXPERT] * gate[:, e:e + 1] for e in range(EXPERTS_PER_GROUP)],
        axis=1).astype(BF16)
    o_ref[...] += _dot(hgated, wd_ref[...])

    if final:
        @pl.when(g == N_GROUPS - 1)
        def _():
            o_ref[...] = _rms(o_ref[...]) * gfin_ref[...]


def _moe(x, gn, wr, br, wg, wu, wd, gfin, *, tm, final):
    t = x.shape[0]
    row = pl.BlockSpec((tm, D_MODEL), lambda i, g: (i, 0))
    const = lambda shape: pl.BlockSpec(shape, lambda i, g: (0,) * len(shape))
    return pl.pallas_call(
        functools.partial(_moe_kernel, final=final), grid=(t // tm, N_GROUPS),
        in_specs=[row, const((1, D_MODEL)), const((D_MODEL, LANES)), const((1, LANES)),
                  pl.BlockSpec((D_MODEL, D_GROUP), lambda i, g: (0, g)),
                  pl.BlockSpec((D_MODEL, D_GROUP), lambda i, g: (0, g)),
                  pl.BlockSpec((D_GROUP, D_MODEL), lambda i, g: (g, 0)),
                  const((1, D_MODEL))],
        out_specs=row, out_shape=jax.ShapeDtypeStruct((t, D_MODEL), F32),
        scratch_shapes=[pltpu.VMEM((tm, D_MODEL), BF16), pltpu.VMEM((N_GROUPS, tm, LANES), F32)],
        compiler_params=_params("arbitrary", "arbitrary"), name="moe",
    )(x, gn, wr, br, wg, wu, wd, gfin)


def _tri(n, rel):
    r = lax.broadcasted_iota(jnp.int32, (n, n), 0)
    c = lax.broadcasted_iota(jnp.int32, (n, n), 1)
    return rel(r, c).astype(BF16)


def kernel(x_prompt, x_sample, cache_k_fox, cache_v_fox, cache_logf_fox, cache_k_sb, cache_v_sb,
           state_conv, page_table, g_mix_norm, w_in, b_f, conv_w, g_mix_out, w_out, g_ffn_norm,
           w_router_g, b_router_g, w_router_e, b_router_e, w_gate, w_up, w_down, g_final):
    depth = w_in.shape[0]
    n_seq, seq, _ = x_prompt.shape
    n_dec = x_sample.shape[0]
    t_p = n_seq * seq
    tm_in, tm_out, tm_moe = 512, 512, 512
    tq_f, tk_f, tq_s, tk_s = 512, 512, 512, 256

    c_fl = 3 * D_ATT
    c_end = c_fl + N_HEADS
    w_in_r = jnp.concatenate(
        [w_in[..., :c_fl], w_in[..., c_end:], w_in[..., c_fl:c_end],
         jnp.zeros(w_in.shape[:2] + (N_IN_PAD - w_in.shape[2],), w_in.dtype)], axis=-1).astype(BF16)
    bf_r = jnp.pad(b_f, ((0, 0), (0, LANES - N_HEADS)))[:, None, :]
    cw_r = jnp.pad(conv_w, ((0, 0), (0, 8 - CONV_WIDTH), (0, 0)))
    g_f = g_mix_out[:, None, :D_ATT]
    g_c = g_mix_out[:, None, D_ATT:D_ATT + D_CONV]
    g_s = g_mix_out[:, None, D_ATT + D_CONV:]
    w_out_r = w_out.astype(BF16)
    w_r = jnp.concatenate(
        [w_router_e, w_router_g,
         jnp.zeros((depth, D_MODEL, LANES - N_EXPERTS - N_GROUPS), F32)], axis=-1)
    b_r = jnp.concatenate(
        [b_router_e, b_router_g, jnp.zeros((depth, LANES - N_EXPERTS - N_GROUPS), F32)],
        axis=-1)[:, None, :]
    w_gate_r = jnp.transpose(w_gate, (0, 2, 1, 3)).reshape(depth, D_MODEL, -1).astype(BF16)
    w_up_r = jnp.transpose(w_up, (0, 2, 1, 3)).reshape(depth, D_MODEL, -1).astype(BF16)
    w_down_r = w_down.reshape(depth, -1, D_MODEL).astype(BF16)
    g_mix = g_mix_norm[:, None, :]
    g_ffn = g_ffn_norm[:, None, :]
    g_fin = g_final[None, :]

    tri_in = _tri(tm_in, lambda r, c: c <= r)
    su_sb = _tri(tk_s, lambda r, c: r > c)
    su_page = _tri(PAGE_SIZE, lambda r, c: r > c)
    cache_lf_t = jnp.pad(jnp.swapaxes(cache_logf_fox, 2, 3), ((0, 0), (0, 0), (0, 8 - N_HEADS), (0, 0)))

    def heads16(a):
        return jnp.pad(a.reshape(n_dec, N_HEADS, HEAD_DIM), ((0, 0), (0, HEAD_ROWS - N_HEADS), (0, 0)))

    xp = x_prompt.reshape(t_p, D_MODEL)
    xs = x_sample.reshape(n_dec, D_MODEL)
    st_p = [[] for _ in range(6)]
    st_s = [[] for _ in range(6)]
    for l in range(depth):
        final = l == depth - 1
        (qf, kf32, vf32, kf16, vf16, qs, ks32, vs32, ks16, vs16, logf, fcum, oc, utail) = _inproj_prompt(
            xp, g_mix[l], w_in_r[l], bf_r[l], cw_r[l], g_c[l], tri_in, seq=seq, tm=tm_in)
        fk = jnp.swapaxes(fcum[:, :N_HEADS].reshape(n_seq, seq, N_HEADS // 2, 2), 1, 3)
        fk = jnp.pad(jnp.swapaxes(fk, 1, 2), ((0, 0), (0, 0), (0, 6), (0, 0)))
        of = _fox_prompt(qf, kf16, vf16, fk, seq=seq, tq=tq_f, tk=tk_f)
        os_ = _sb_prompt(qs, ks16, vs16, su_sb, seq=seq, tq=tq_s, tk=tk_s)
        xp = _outproj(of, oc, os_, xp, g_f[l], g_s[l], w_out_r[l], tm=tm_out)
        xp = _moe(xp, g_ffn[l], w_r[l], b_r[l], w_gate_r[l], w_up_r[l], w_down_r[l], g_fin,
                  tm=tm_moe, final=final)
        hd = lambda a: a.reshape(n_seq, seq, N_HEADS, HEAD_DIM)
        new_p = (hd(kf32), hd(vf32), logf[:, :N_HEADS].reshape(n_seq, seq, N_HEADS), hd(ks32), hd(vs32),
                 utail.reshape(n_seq, 8, D_CONV)[:, 8 - (CONV_WIDTH - 1):])
        prev = state_conv[l]
        (qf, kf32, vf32, qs, ks32, vs32, logf, oc, u) = _inproj_sample(
            xs, g_mix[l], w_in_r[l], bf_r[l], cw_r[l], g_c[l], prev[:, 0], prev[:, 1])
        lfn = jnp.pad(logf[:, :N_HEADS], ((0, 0), (0, HEAD_ROWS - N_HEADS)))[:, :, None]
        of, os_ = _decode_attn(l, page_table, heads16(qf), heads16(kf32), heads16(vf32), lfn,
                               heads16(qs), su_page,
                               cache_k_fox, cache_v_fox, cache_lf_t, cache_k_sb, cache_v_sb)
        of = of[:, :N_HEADS].reshape(n_dec, D_ATT)
        os_ = os_[:, :N_HEADS].reshape(n_dec, D_ATT)
        xs = _outproj(of, oc, os_, xs, g_f[l], g_s[l], w_out_r[l], tm=n_dec)
        xs = _moe(xs, g_ffn[l], w_r[l], b_r[l], w_gate_r[l], w_up_r[l], w_down_r[l], g_fin,
                  tm=n_dec, final=final)
        hd1 = lambda a: a.reshape(n_dec, 1, N_HEADS, HEAD_DIM)
        new_s = (hd1(kf32), hd1(vf32), logf[:, :N_HEADS].reshape(n_dec, 1, N_HEADS), hd1(ks32), hd1(vs32),
                 jnp.stack([prev[:, 1], u], axis=1))
        for acc, a in zip(st_p, new_p):
            acc.append(a)
        for acc, a in zip(st_s, new_s):
            acc.append(a)

    y_prompt = xp.reshape(n_seq, seq, D_MODEL)
    y_sample = xs.reshape(n_dec, 1, D_MODEL)
    return (y_prompt, y_sample) + tuple(jnp.stack(a) for a in st_p) + tuple(jnp.stack(a) for a in st_s)
```

```python
import functools
import math

import jax
import jax.numpy as jnp
from jax import lax
from jax.experimental import pallas as pl
from jax.experimental.pallas import tpu as pltpu

D_MODEL = 1024
HEAD_DIM = 64
N_HEADS = 6
N_PAIRS = N_HEADS // 2
D_ATT = N_HEADS * HEAD_DIM
D_CONV = 256
CONV_WIDTH = 3
N_GROUPS = 4
EXPERTS_PER_GROUP = 8
N_EXPERTS = N_GROUPS * EXPERTS_PER_GROUP
D_EXPERT = 128
D_GROUP = EXPERTS_PER_GROUP * D_EXPERT
PAGE_SIZE = 128
EPS = 1e-6
SCALE = HEAD_DIM ** -0.5
LOG2E = math.log2(math.e)
NEG = -1e30

LANES = 128
HEAD_ROWS = 16
D_SLOT = N_HEADS * LANES
PAGES_PER_STEP = 8
VMEM_LIMIT = 56 * 1024 * 1024

C_QF, C_KF, C_VF = 0, 384, 768
C_GB, C_GC, C_HC = 1152, 1408, 1664
C_QS, C_KS, C_VS = 1920, 2304, 2688
C_FL = 3072
N_IN_PAD = 3200
P_KA, P_CV, P_QS, P_VS, P_FL, P_END = 0, 768, 1536, 1920, 2304, 2432
T_QF, T_KF, T_VF, T_KS, T_VS, T_FL, T_END = 0, 384, 768, 1152, 1536, 1920, 1936

F32 = jnp.float32
BF16 = jnp.bfloat16


def _rms(x):
    return x * lax.rsqrt(jnp.sum(x * x, axis=-1, keepdims=True) * (1.0 / x.shape[-1]) + EPS)


def _softplus(z):
    return jnp.maximum(z, 0.0) + jnp.log(1.0 + jnp.exp(-jnp.abs(z)))


def _split3(x):
    hi = x.astype(BF16)
    r = x - hi.astype(F32)
    mid = r.astype(BF16)
    lo = (r - mid.astype(F32)).astype(BF16)
    return hi, mid, lo


def _split2(x):
    hi = x.astype(BF16)
    return hi, (x - hi.astype(F32)).astype(BF16)


def _dot(a, b):
    return jnp.dot(a, b, preferred_element_type=F32)


def _dot_nt(a, b):
    return lax.dot_general(a, b, (((1,), (1,)), ((), ())), preferred_element_type=F32)


def _params(*sem):
    return pltpu.CompilerParams(dimension_semantics=sem, vmem_limit_bytes=VMEM_LIMIT)


def _full(shape):
    return pl.BlockSpec(shape, lambda *_: (0,) * len(shape))


def _inproj_prompt_kernel(*refs, tm, tiles_per_seq, n_alias):
    (x_ref, gn_ref, wnn_ref, wnt_ref, bf_ref, bft_ref, cw_ref, gc_ref, tri_ref,
     place_ref) = refs[:10]
    (qtf_ref, kaug_ref, vtf16_ref, qs_ref, kts16_ref, vs16_ref, oc_ref, utail_ref,
     ktf32_ref, vtf32_ref, kts32_ref, vts32_ref, lft_ref, ubuf, fcarry) = refs[10 + n_alias:]
    i = pl.program_id(0)
    xb = (_rms(x_ref[...]) * gn_ref[...]).astype(BF16)

    def nn(a, b):
        return _dot(xb, wnn_ref[:, a:b])

    def nt(a, b):
        return _dot_nt(wnt_ref[a:b, :], xb)

    @pl.when(i % tiles_per_seq == 0)
    def _():
        ubuf[0:8, :] = jnp.zeros((8, D_CONV), F32)
        fcarry[...] = jnp.zeros((1, LANES), F32)

    fl = nn(P_FL, P_END) + bf_ref[...]
    lane = lax.broadcasted_iota(jnp.int32, fl.shape, 1)
    logf = jnp.where(lane < N_HEADS, -_softplus(-fl), 0.0)
    tri = tri_ref[...]
    hi, mid, lo = _split3(logf)
    fcum = _dot(tri, hi) + _dot(tri, mid) + _dot(tri, lo) + fcarry[...]
    fcarry[...] = fcum[tm - 1:tm, :]
    lft_ref[...] = -_softplus(-(nt(T_FL, T_END) + bft_ref[...]))[0:8, :]

    hi, mid, lo = _split3(fcum * (-LOG2E))
    kaug = (nn(P_KA, P_CV) + _dot(hi, place_ref[0]) + _dot(mid, place_ref[1])
            + _dot(lo, place_ref[2]))
    kaug_ref[...] = kaug.astype(BF16)
    qt = (nt(T_QF, T_KF) * (SCALE * LOG2E)).astype(BF16)
    row = lax.broadcasted_iota(jnp.int32, (HEAD_DIM, tm), 0)
    ones_rows = jnp.where(row < 3, 1.0, 0.0).astype(BF16)
    for h in range(N_HEADS):
        qtf_ref[h * LANES:h * LANES + HEAD_DIM, :] = qt[h * HEAD_DIM:(h + 1) * HEAD_DIM, :]
        qtf_ref[h * LANES + HEAD_DIM:(h + 1) * LANES, :] = ones_rows
    ktf32_ref[...] = nt(T_KF, T_VF)
    vt = nt(T_VF, T_KS)
    vtf32_ref[...] = vt
    vtf16_ref[...] = vt.astype(BF16)

    qs_ref[...] = (nn(P_QS, P_VS) * (SCALE * LOG2E)).astype(BF16)
    vs16_ref[...] = nn(P_VS, P_FL).astype(BF16)
    kt = nt(T_KS, T_VS)
    kts32_ref[...] = kt
    kts16_ref[...] = kt.astype(BF16)
    vts32_ref[...] = nt(T_VS, T_FL)

    gch = nn(P_CV, P_QS)
    u = gch[:, D_CONV:2 * D_CONV] * gch[:, 2 * D_CONV:]
    ubuf[8:tm + 8, :] = u
    y = (ubuf[6:tm + 6, :] * cw_ref[0:1, :] + ubuf[7:tm + 7, :] * cw_ref[1:2, :]
         + u * cw_ref[2:3, :])
    tail = u[tm - 8:tm, :]
    ubuf[0:8, :] = tail
    utail_ref[...] = tail
    oc_ref[...] = (_rms(gch[:, :D_CONV] * y) * gc_ref[...]).astype(BF16)


def _inproj_prompt(layer, depth, x, gn, wnn, wnt, bf, bft, cw, gc, tri, place, stacked, *, seq, tm):
    t = x.shape[0]
    n_seq = t // seq
    tps = seq // tm
    row = lambda width: pl.BlockSpec((tm, width), lambda i: (i, 0))
    colt = lambda rows: pl.BlockSpec((rows, tm), lambda i: (0, i))
    seqt = lambda rows: pl.BlockSpec((None, rows, tm), lambda i: (i // tps, 0, i % tps))
    stk = lambda rows: pl.BlockSpec((None, None, rows, tm), lambda i: (layer, i // tps, 0, i % tps))
    stacked_shapes = [jax.ShapeDtypeStruct((depth, n_seq, D_ATT, seq), F32)] * 4 + [
        jax.ShapeDtypeStruct((depth, 8, t), F32)]
    stacked_specs = [stk(D_ATT)] * 4 + [pl.BlockSpec((None, 8, tm), lambda i: (layer, 0, i))]
    out_shape = [jax.ShapeDtypeStruct((D_SLOT, t), BF16), jax.ShapeDtypeStruct((t, D_SLOT), BF16),
                 jax.ShapeDtypeStruct((n_seq, D_ATT, seq), BF16), jax.ShapeDtypeStruct((t, D_ATT), BF16),
                 jax.ShapeDtypeStruct((n_seq, D_ATT, seq), BF16), jax.ShapeDtypeStruct((t, D_ATT), BF16),
                 jax.ShapeDtypeStruct((t, D_CONV), BF16),
                 jax.ShapeDtypeStruct((n_seq * 8, D_CONV), F32)] + stacked_shapes
    out_specs = [colt(D_SLOT), row(D_SLOT), seqt(D_ATT), row(D_ATT), seqt(D_ATT), row(D_ATT),
                 row(D_CONV), pl.BlockSpec((8, D_CONV), lambda i: (i // tps, 0))] + stacked_specs
    in_specs = [row(D_MODEL), _full((1, D_MODEL)), _full((D_MODEL, P_END)), _full((T_END, D_MODEL)),
                _full((1, LANES)), _full((HEAD_ROWS, 1)), _full((8, D_CONV)), _full((1, D_CONV)),
                _full((tm, tm)), _full((3, LANES, D_SLOT))]
    args = [x, gn, wnn, wnt, bf, bft, cw, gc, tri, place]
    aliases = {}
    n_alias = 0
    if stacked is not None:
        n_alias = len(stacked)
        in_specs += [pl.BlockSpec(memory_space=pl.ANY)] * n_alias
        aliases = {len(args) + k: 8 + k for k in range(n_alias)}
        args += list(stacked)
    return pl.pallas_call(
        functools.partial(_inproj_prompt_kernel, tm=tm, tiles_per_seq=tps, n_alias=n_alias),
        grid=(t // tm,), in_specs=in_specs, out_specs=out_specs, out_shape=out_shape,
        input_output_aliases=aliases,
        scratch_shapes=[pltpu.VMEM((tm + 8, D_CONV), F32), pltpu.VMEM((1, LANES), F32)],
        compiler_params=_params("arbitrary"), name="inproj_prompt",
    )(*args)


def _inproj_sample_kernel(x_ref, gn_ref, w_ref, bf_ref, cw_ref, gc_ref, p0_ref, p1_ref,
                          qf_ref, kf32_ref, vf32_ref, qs_ref, ks32_ref, vs32_ref,
                          logf_ref, oc_ref, u_ref):
    xb = (_rms(x_ref[...]) * gn_ref[...]).astype(BF16)

    def proj(a, b):
        return _dot(xb, w_ref[:, a:b])

    for c_q, q_ref, k_ref, v_ref in ((C_QF, qf_ref, kf32_ref, vf32_ref),
                                     (C_QS, qs_ref, ks32_ref, vs32_ref)):
        qkv = proj(c_q, c_q + 3 * D_ATT)
        q_ref[...] = qkv[:, :D_ATT] * SCALE
        k_ref[...] = qkv[:, D_ATT:2 * D_ATT]
        v_ref[...] = qkv[:, 2 * D_ATT:]
    fl = proj(C_FL, N_IN_PAD) + bf_ref[...]
    lane = lax.broadcasted_iota(jnp.int32, fl.shape, 1)
    logf_ref[...] = jnp.where(lane < N_HEADS, -_softplus(-fl), 0.0)
    gch = proj(C_GB, C_GB + 3 * D_CONV)
    u = gch[:, D_CONV:2 * D_CONV] * gch[:, 2 * D_CONV:]
    u_ref[...] = u
    y = p0_ref[...] * cw_ref[0:1, :] + p1_ref[...] * cw_ref[1:2, :] + u * cw_ref[2:3, :]
    oc_ref[...] = (_rms(gch[:, :D_CONV] * y) * gc_ref[...]).astype(BF16)


def _inproj_sample(x, gn, w, bf, cw, gc, p0, p1):
    t = x.shape[0]
    att = jax.ShapeDtypeStruct((t, D_ATT), F32)
    out_shape = [att] * 6 + [jax.ShapeDtypeStruct((t, LANES), F32),
                             jax.ShapeDtypeStruct((t, D_CONV), BF16),
                             jax.ShapeDtypeStruct((t, D_CONV), F32)]
    return pl.pallas_call(
        _inproj_sample_kernel, out_shape=out_shape,
        compiler_params=pltpu.CompilerParams(vmem_limit_bytes=VMEM_LIMIT), name="inproj_sample",
    )(x, gn, w, bf, cw, gc, p0, p1)


def _fox_prompt_kernel(qt_ref, k_ref, vt_ref, o_ref, sa_s, sb_s, m_s, l_s, acc_s, *, tq, tk):
    qi = pl.program_id(2)
    m_s[...] = jnp.full(m_s.shape, NEG, F32)
    l_s[...] = jnp.zeros(l_s.shape, F32)
    acc_s[...] = jnp.zeros(acc_s.shape, F32)

    def scores(j, s_ref):
        start = pl.multiple_of(j * tk, tk)
        for h in range(2):
            k = k_ref[pl.ds(start, tk), h * LANES:(h + 1) * LANES]
            s_ref[h] = _dot(k, qt_ref[h * LANES:(h + 1) * LANES, :])

    def update(j, s_ref, masked):
        start = pl.multiple_of(j * tk, tk)
        if masked:
            key = lax.broadcasted_iota(jnp.int32, (tk, tq), 0) + start
            qry = lax.broadcasted_iota(jnp.int32, (tk, tq), 1) + qi * tq
            causal = key <= qry
        for h in range(2):
            s = s_ref[h]
            if masked:
                s = jnp.where(causal, s, NEG)
            m_old = m_s[h]
            m_new = jnp.maximum(m_old, jnp.max(s, axis=0, keepdims=True))
            alpha = jnp.exp2(m_old - m_new)
            p = jnp.exp2(s - m_new)
            l_s[h] = alpha * l_s[h] + jnp.sum(p, axis=0, keepdims=True)
            vt = vt_ref[h * HEAD_DIM:(h + 1) * HEAD_DIM, pl.ds(start, tk)]
            acc_s[h] = alpha * acc_s[h] + _dot(vt, p.astype(BF16))
            m_s[h] = m_new

    assert tq == 2 * tk
    scores(0, sa_s)

    def pair(i, carry):
        scores(2 * i + 1, sb_s)
        update(2 * i, sa_s, False)
        scores(2 * i + 2, sa_s)
        update(2 * i + 1, sb_s, False)
        return carry

    lax.fori_loop(0, qi, pair, 0)
    scores(2 * qi + 1, sb_s)
    update(2 * qi, sa_s, True)
    update(2 * qi + 1, sb_s, True)

    o_t = jnp.concatenate([acc_s[0] / l_s[0], acc_s[1] / l_s[1]], axis=0)
    o_ref[...] = o_t.T


def _split_heads(q_ref, q_s):
    q = q_ref[...]
    lo = lax.broadcasted_iota(jnp.int32, q.shape, 1) < HEAD_DIM
    zero = jnp.zeros_like(q)
    q_s[0] = jnp.where(lo, q, zero)
    q_s[1] = jnp.where(lo, zero, q)


def _sb_prompt_kernel(q_ref, kt_ref, v_ref, su_ref, o_ref, q_s, za_s, zb_s, r_s, acc_s, *, tq, tk):
    qi = pl.program_id(2)
    _split_heads(q_ref, q_s)
    r_s[...] = jnp.zeros(r_s.shape, F32)
    acc_s[...] = jnp.zeros(acc_s.shape, F32)

    def logits(j, z_ref):
        start = pl.multiple_of(jnp.maximum(j, 0) * tk, tk)
        kt = kt_ref[:, pl.ds(start, tk)]
        for h in range(2):
            z_ref[h] = _dot(q_s[h], kt)

    def update(j, z_ref, masked):
        start = pl.multiple_of(j * tk, tk)
        v = v_ref[pl.ds(start, tk), :]
        su = su_ref[...]
        if masked:
            row = lax.broadcasted_iota(jnp.int32, (tq, tk), 0) + qi * tq
            col = lax.broadcasted_iota(jnp.int32, (tq, tk), 1) + start
            strict = col < row
        exponents = []
        for h in range(2):
            z = z_ref[h]
            ls = jnp.minimum(z, 0.0) - jnp.log2(1.0 + jnp.exp2(-jnp.abs(z)))
            lk = ls - z
            if masked:
                lk = jnp.where(strict, lk, 0.0)
            hi, lo = _split2(lk)
            r_old = r_s[h]
            after = _dot(hi, su) + _dot(lo, su) + jnp.concatenate([r_old] * (tk // LANES), axis=1)
            exponents.append(ls + after)
            r_s[h] = r_old + jnp.sum(lk, axis=-1, keepdims=True)
        for h in range(2):
            a = jnp.exp2(exponents[h])
            if masked:
                a = jnp.where(strict, a, 0.0)
            acc_s[h] = acc_s[h] + _dot(a.astype(BF16), v)

    assert tq == 2 * tk
    top = 2 * qi + 1
    logits(top, za_s)
    logits(top - 1, zb_s)
    update(top, za_s, True)
    logits(top - 2, za_s)
    update(top - 1, zb_s, True)

    def pair(i, carry):
        j = top - 2 - 2 * i
        logits(j - 1, zb_s)
        update(j, za_s, False)
        logits(j - 2, za_s)
        update(j - 1, zb_s, False)
        return carry

    lax.fori_loop(0, qi, pair, 0)

    lo = lax.broadcasted_iota(jnp.int32, (tq, LANES), 1) < HEAD_DIM
    o_ref[...] = jnp.where(lo, acc_s[0], acc_s[1])


def _fox_prompt(qt, kaug, vt, *, seq, tq, tk):
    t = kaug.shape[0]
    n_seq, nq = t // seq, seq // tq
    o_spec = pl.BlockSpec((tq, LANES), lambda b, p, i: (b * nq + i, p))
    return pl.pallas_call(
        functools.partial(_fox_prompt_kernel, tq=tq, tk=tk), grid=(n_seq, N_PAIRS, nq),
        in_specs=[pl.BlockSpec((2 * LANES, tq), lambda b, p, i: (p, b * nq + i)),
                  pl.BlockSpec((seq, 2 * LANES), lambda b, p, i: (b, p)),
                  pl.BlockSpec((None, LANES, seq), lambda b, p, i: (b, p, 0))],
        out_specs=o_spec, out_shape=jax.ShapeDtypeStruct((t, D_ATT), F32),
        scratch_shapes=[pltpu.VMEM((2, tk, tq), F32), pltpu.VMEM((2, tk, tq), F32),
                        pltpu.VMEM((2, 1, tq), F32), pltpu.VMEM((2, 1, tq), F32),
                        pltpu.VMEM((2, HEAD_DIM, tq), F32)],
        compiler_params=_params("arbitrary", "arbitrary", "arbitrary"), name="fox_prompt",
    )(qt, kaug, vt)


def _sb_prompt(q, kt, v, su, *, seq, tq, tk):
    t = q.shape[0]
    n_seq, nq = t // seq, seq // tq
    q_spec = pl.BlockSpec((tq, LANES), lambda b, p, i: (b * nq + i, p))
    return pl.pallas_call(
        functools.partial(_sb_prompt_kernel, tq=tq, tk=tk), grid=(n_seq, N_PAIRS, nq),
        in_specs=[q_spec,
                  pl.BlockSpec((None, LANES, seq), lambda b, p, i: (b, p, 0)),
                  pl.BlockSpec((seq, LANES), lambda b, p, i: (b, p)),
                  pl.BlockSpec((tk, tk), lambda b, p, i: (0, 0))],
        out_specs=q_spec, out_shape=jax.ShapeDtypeStruct((t, D_ATT), F32),
        scratch_shapes=[pltpu.VMEM((2, tq, LANES), BF16), pltpu.VMEM((2, tq, tk), F32),
                        pltpu.VMEM((2, tq, tk), F32), pltpu.VMEM((2, tq, LANES), F32),
                        pltpu.VMEM((2, tq, LANES), F32)],
        compiler_params=_params("arbitrary", "arbitrary", "arbitrary"), name="sb_prompt",
    )(q, kt, v, su)


def _pick_rows(parts):
    rows = lax.broadcasted_iota(jnp.int32, parts[0].shape, 0)
    out = jnp.zeros_like(parts[0])
    for h, part in enumerate(parts):
        out = jnp.where(rows == h, part, out)
    return out


def _suffix_over_pages(x, su, carry):
    g_n = x.shape[1] // PAGE_SIZE
    stacked = jnp.concatenate([x[:, g * PAGE_SIZE:(g + 1) * PAGE_SIZE] for g in range(g_n)], axis=0)
    hi, mid, lo = _split3(stacked)
    inside = _dot(hi, su) + _dot(mid, su) + _dot(lo, su)
    total = jnp.sum(stacked, axis=-1, keepdims=True)
    outs = []
    for g in range(g_n):
        outs.append(inside[g * HEAD_ROWS:(g + 1) * HEAD_ROWS] + carry)
        carry = carry + total[g * HEAD_ROWS:(g + 1) * HEAD_ROWS]
    return jnp.concatenate(outs, axis=1), carry


def _decode_attn_kernel(pt_ref, qf_ref, knf_ref, vnf_ref, lfn_ref, qs_ref, su_ref, *refs):
    del pt_ref
    g_n = PAGES_PER_STEP
    kf, vf, lf, ks, vs = (refs[k * g_n:(k + 1) * g_n] for k in range(5))
    of_ref, os_ref, m_s, l_s, g_s, accf_s, r_s, accs_s = refs[5 * g_n:]
    p = pl.program_id(1)

    @pl.when(p == 0)
    def _():
        m_s[...] = jnp.full(m_s.shape, NEG, F32)
        l_s[...] = jnp.zeros(l_s.shape, F32)
        g_s[...] = jnp.zeros(g_s.shape, F32)
        r_s[...] = jnp.zeros(r_s.shape, F32)
        accf_s[...] = jnp.zeros(accf_s.shape, F32)
        accs_s[...] = jnp.zeros(accs_s.shape, F32)

    su = su_ref[...]
    qf = qf_ref[...]

    def wide(page_refs, h):
        return jnp.concatenate([r[h].astype(BF16) for r in page_refs], axis=1)

    def scores(q, page_refs):
        qb = q.astype(BF16)
        return _pick_rows([_dot(qb, wide(page_refs, h)) for h in range(N_HEADS)])

    zeros = jnp.zeros((HEAD_ROWS - 8, PAGE_SIZE), F32)
    lf_w = jnp.concatenate([jnp.concatenate([r[...], zeros], axis=0) for r in lf], axis=1)
    decay, g_new = _suffix_over_pages(lf_w, su, g_s[...])
    g_s[...] = g_new
    s = scores(qf, kf) + decay + lfn_ref[...]
    m_old = m_s[...]
    m_new = jnp.maximum(m_old, jnp.max(s, axis=-1, keepdims=True))
    alpha = jnp.exp(m_old - m_new)
    pr = jnp.exp(s - m_new)
    l_s[...] = alpha * l_s[...] + jnp.sum(pr, axis=-1, keepdims=True)
    m_s[...] = m_new
    prb = pr.astype(BF16)
    for h in range(N_HEADS):
        accf_s[h] = alpha * accf_s[h] + _dot_nt(prb, wide(vf, h))

    z = scores(qs_ref[...], ks)
    lk = -_softplus(z)
    after, r_new = _suffix_over_pages(lk, su, r_s[...])
    r_s[...] = r_new
    ab = jnp.exp(z + lk + after).astype(BF16)
    for h in range(N_HEADS):
        accs_s[h] = accs_s[h] + _dot_nt(ab, wide(vs, h))

    @pl.when(p == pl.num_programs(1) - 1)
    def _():
        s_self = jnp.sum(qf * knf_ref[...], axis=-1, keepdims=True)
        m_old = m_s[...]
        m_new = jnp.maximum(m_old, s_self)
        alpha = jnp.exp(m_old - m_new)
        p_self = jnp.exp(s_self - m_new)
        inv_l = 1.0 / (alpha * l_s[...] + p_self)
        own = p_self * vnf_ref[...]
        of_ref[...] = _pick_rows([alpha * accf_s[h] + own for h in range(N_HEADS)]) * inv_l
        os_ref[...] = _pick_rows([accs_s[h] for h in range(N_HEADS)])


def _decode_attn(layer, page_table, qf, knf, vnf, lfn, qs, su,
                 cache_kf, cache_vf, cache_lf, cache_ks, cache_vs):
    n_b, n_pages = page_table.shape
    g_n = PAGES_PER_STEP
    per_b = lambda shape: pl.BlockSpec((None,) + shape, lambda b, p, pt: (b,) + (0,) * len(shape))

    def pages(shape):
        return [pl.BlockSpec((None, None) + shape,
                             lambda b, p, pt, g=g: (layer, pt[b, n_pages - 1 - (p * g_n + g)])
                             + (0,) * len(shape)) for g in range(g_n)]

    head_blk = per_b((HEAD_ROWS, HEAD_DIM))
    kv_pages = pages((N_HEADS, HEAD_DIM, PAGE_SIZE))
    out = jax.ShapeDtypeStruct((n_b, HEAD_ROWS, HEAD_DIM), F32)
    col = pltpu.VMEM((HEAD_ROWS, 1), F32)
    acc = pltpu.VMEM((N_HEADS, HEAD_ROWS, HEAD_DIM), F32)
    return pl.pallas_call(
        _decode_attn_kernel,
        grid_spec=pltpu.PrefetchScalarGridSpec(
            num_scalar_prefetch=1, grid=(n_b, n_pages // g_n),
            in_specs=[head_blk, head_blk, head_blk, per_b((HEAD_ROWS, 1)), head_blk,
                      pl.BlockSpec((PAGE_SIZE, PAGE_SIZE), lambda b, p, pt: (0, 0))]
            + kv_pages + kv_pages + pages((8, PAGE_SIZE)) + kv_pages + kv_pages,
            out_specs=[head_blk, head_blk],
            scratch_shapes=[col, col, col, acc, col, acc]),
        out_shape=[out, out],
        compiler_params=_params("arbitrary", "arbitrary"), name="decode_attn",
    )(page_table, qf, knf, vnf, lfn, qs, su, *([cache_kf] * g_n), *([cache_vf] * g_n),
      *([cache_lf] * g_n), *([cache_ks] * g_n), *([cache_vs] * g_n))


def _outproj_kernel(of_ref, oc_ref, os_ref, x_ref, gf_ref, gs_ref, w_ref, o_ref):
    nf = (_rms(of_ref[...]) * gf_ref[...]).astype(BF16)
    ns = (_rms(os_ref[...]) * gs_ref[...]).astype(BF16)
    y = (_dot(nf, w_ref[0:D_ATT, :]) + _dot(oc_ref[...], w_ref[D_ATT:D_ATT + D_CONV, :])
         + _dot(ns, w_ref[D_ATT + D_CONV:, :]))
    o_ref[...] = x_ref[...] + y


def _outproj(of, oc, os_, x, gf, gs, w, *, tm):
    t = x.shape[0]
    row = lambda width: pl.BlockSpec((tm, width), lambda i: (i, 0))
    return pl.pallas_call(
        _outproj_kernel, grid=(t // tm,),
        in_specs=[row(D_ATT), row(D_CONV), row(D_ATT), row(D_MODEL), _full((1, D_ATT)),
                  _full((1, D_ATT)), _full((D_MODEL, D_MODEL))],
        out_specs=row(D_MODEL), out_shape=jax.ShapeDtypeStruct((t, D_MODEL), F32),
        compiler_params=_params("arbitrary"), name="outproj",
    )(of, oc, os_, x, gf, gs, w)


def _route(logits):
    lane_i = lax.broadcasted_iota(jnp.int32, logits.shape, 1)
    lane = lane_i.astype(F32)
    lane_group = jnp.right_shift(lane_i, EXPERTS_PER_GROUP.bit_length() - 1).astype(F32)
    big = float(LANES)
    is_group = (lane_i >= N_EXPERTS) & (lane_i < N_EXPERTS + N_GROUPS)
    gl = jnp.where(is_group, logits, NEG)
    g_max = jnp.max(gl, axis=-1, keepdims=True)
    g_idx = jnp.min(jnp.where(gl == g_max, lane, big), axis=-1, keepdims=True) - N_EXPERTS
    p_group = 1.0 / jnp.sum(jnp.exp(gl - g_max), axis=-1, keepdims=True)
    in_group = (lane_i < N_EXPERTS) & (lane_group == g_idx)
    el = jnp.where(in_group, logits, NEG)
    m1 = jnp.max(el, axis=-1, keepdims=True)
    i1 = jnp.min(jnp.where(el == m1, lane, big), axis=-1, keepdims=True)
    el2 = jnp.where(lane == i1, NEG, el)
    m2 = jnp.max(el2, axis=-1, keepdims=True)
    i2 = jnp.min(jnp.where(el2 == m2, lane, big), axis=-1, keepdims=True)
    e21 = jnp.exp(m2 - m1)
    w1 = p_group / (1.0 + e21)
    return jnp.where(lane == i1, w1, 0.0) + jnp.where(lane == i2, w1 * e21, 0.0)


def _moe_kernel(x_ref, gn_ref, wr_ref, br_ref, wg_ref, wu_ref, wd_ref, gfin_ref, o_ref,
                xn_s, gate_s, *, final):
    g = pl.program_id(1)

    @pl.when(g == 0)
    def _():
        x = x_ref[...]
        xn = _rms(x) * gn_ref[...]
        xn_s[...] = xn.astype(BF16)
        logits = jnp.dot(xn, wr_ref[...], preferred_element_type=F32,
                         precision=lax.Precision.HIGHEST) + br_ref[...]
        gate = _route(logits)
        for gi in range(N_GROUPS):
            shift = (LANES - gi * EXPERTS_PER_GROUP) % LANES
            gate_s[gi] = pltpu.roll(gate, shift, 1) if shift else gate
        o_ref[...] = x

    xb = xn_s[...]
    hg = _dot(xb, wg_ref[...])
    hu = _dot(xb, wu_ref[...])
    h = hg * (1.0 / (1.0 + jnp.exp(-hg))) * hu
    gate = gate_s[g]
    hgated = jnp.concatenate(
        [h[:, e * D_EXPERT:(e + 1) * D_EXPERT] * gate[:, e:e + 1] for e in range(EXPERTS_PER_GROUP)],
        axis=1).astype(BF16)
    o_ref[...] += _dot(hgated, wd_ref[...])

    if final:
        @pl.when(g == N_GROUPS - 1)
        def _():
            o_ref[...] = _rms(o_ref[...]) * gfin_ref[...]


def _moe(x, gn, wr, br, wg, wu, wd, gfin, *, tm, final):
    t = x.shape[0]
    row = pl.BlockSpec((tm, D_MODEL), lambda i, g: (i, 0))
    const = lambda shape: pl.BlockSpec(shape, lambda i, g: (0,) * len(shape))
    return pl.pallas_call(
        functools.partial(_moe_kernel, final=final), grid=(t // tm, N_GROUPS),
        in_specs=[row, const((1, D_MODEL)), const((D_MODEL, LANES)), const((1, LANES)),
                  pl.BlockSpec((D_MODEL, D_GROUP), lambda i, g: (0, g)),
                  pl.BlockSpec((D_MODEL, D_GROUP), lambda i, g: (0, g)),
                  pl.BlockSpec((D_GROUP, D_MODEL), lambda i, g: (g, 0)),
                  const((1, D_MODEL))],
        out_specs=row, out_shape=jax.ShapeDtypeStruct((t, D_MODEL), F32),
        scratch_shapes=[pltpu.VMEM((tm, D_MODEL), BF16), pltpu.VMEM((N_GROUPS, tm, LANES), F32)],
        compiler_params=_params("arbitrary", "arbitrary"), name="moe",
    )(x, gn, wr, br, wg, wu, wd, gfin)


def _tri(n, rel):
    r = lax.broadcasted_iota(jnp.int32, (n, n), 0)
    c = lax.broadcasted_iota(jnp.int32, (n, n), 1)
    return rel(r, c).astype(BF16)


def kernel(x_prompt, x_sample, cache_k_fox, cache_v_fox, cache_logf_fox, cache_k_sb, cache_v_sb,
           state_conv, page_table, g_mix_norm, w_in, b_f, conv_w, g_mix_out, w_out, g_ffn_norm,
           w_router_g, b_router_g, w_router_e, b_router_e, w_gate, w_up, w_down, g_final):
    depth = w_in.shape[0]
    n_seq, seq, _ = x_prompt.shape
    n_dec = x_sample.shape[0]
    t_p = n_seq * seq
    tm_in, tm_out, tm_moe = 256, 512, 512
    tq_f, tk_f, tq_s, tk_s = 512, 256, 512, 256

    c_fl = 3 * D_ATT
    c_end = c_fl + N_HEADS
    wq_f, wk_f, wv_f = w_in[..., :D_ATT], w_in[..., D_ATT:2 * D_ATT], w_in[..., 2 * D_ATT:c_fl]
    w_f = w_in[..., c_fl:c_end]
    w_cv = w_in[..., c_end:c_end + 3 * D_CONV]
    c_s = c_end + 3 * D_CONV
    wq_s, wk_s, wv_s = (w_in[..., c_s:c_s + D_ATT], w_in[..., c_s + D_ATT:c_s + 2 * D_ATT],
                        w_in[..., c_s + 2 * D_ATT:])
    pad_cols = lambda w, n: jnp.pad(w, ((0, 0), (0, 0), (0, n - w.shape[-1])))
    w_in_r = jnp.concatenate([wq_f, wk_f, wv_f, w_cv, wq_s, wk_s, wv_s, pad_cols(w_f, LANES)],
                             axis=-1).astype(BF16)
    wk_slots = jnp.pad(wk_f.reshape(depth, D_MODEL, N_HEADS, HEAD_DIM),
                       ((0, 0), (0, 0), (0, 0), (0, LANES - HEAD_DIM))).reshape(depth, D_MODEL, D_SLOT)
    w_nn = jnp.concatenate([wk_slots, w_cv, wq_s, wv_s, pad_cols(w_f, LANES)], axis=-1).astype(BF16)
    w_nt = jnp.swapaxes(jnp.concatenate([wq_f, wk_f, wv_f, wk_s, wv_s, pad_cols(w_f, HEAD_ROWS)],
                                        axis=-1), 1, 2).astype(BF16)
    bf_r = jnp.pad(b_f, ((0, 0), (0, LANES - N_HEADS)))[:, None, :]
    bft_r = jnp.pad(b_f, ((0, 0), (0, HEAD_ROWS - N_HEADS)))[:, :, None]
    cw_r = jnp.pad(conv_w, ((0, 0), (0, 8 - CONV_WIDTH), (0, 0)))
    g_f = g_mix_out[:, None, :D_ATT]
    g_c = g_mix_out[:, None, D_ATT:D_ATT + D_CONV]
    g_s = g_mix_out[:, None, D_ATT + D_CONV:]
    w_out_r = w_out.astype(BF16)
    w_r = jnp.concatenate(
        [w_router_e, w_router_g,
         jnp.zeros((depth, D_MODEL, LANES - N_EXPERTS - N_GROUPS), F32)], axis=-1)
    b_r = jnp.concatenate(
        [b_router_e, b_router_g, jnp.zeros((depth, LANES - N_EXPERTS - N_GROUPS), F32)],
        axis=-1)[:, None, :]
    w_gate_r = jnp.transpose(w_gate, (0, 2, 1, 3)).reshape(depth, D_MODEL, -1).astype(BF16)
    w_up_r = jnp.transpose(w_up, (0, 2, 1, 3)).reshape(depth, D_MODEL, -1).astype(BF16)
    w_down_r = w_down.reshape(depth, -1, D_MODEL).astype(BF16)
    g_mix = g_mix_norm[:, None, :]
    g_ffn = g_ffn_norm[:, None, :]
    g_fin = g_final[None, :]

    tri_in = _tri(tm_in, lambda r, c: c <= r)
    su_sb = _tri(tk_s, lambda r, c: r > c)
    su_page = _tri(PAGE_SIZE, lambda r, c: r > c)
    pr = lax.broadcasted_iota(jnp.int32, (3, LANES, D_SLOT), 1)
    pc = lax.broadcasted_iota(jnp.int32, (3, LANES, D_SLOT), 2)
    pi = lax.broadcasted_iota(jnp.int32, (3, LANES, D_SLOT), 0)
    place = ((pr < N_HEADS) & (pc == pr * LANES + HEAD_DIM + pi)).astype(BF16)

    to_lanes = lambda c: jnp.transpose(c, (0, 1, 3, 4, 2))
    ck_f, cv_f, ck_s, cv_s = map(to_lanes, (cache_k_fox, cache_v_fox, cache_k_sb, cache_v_sb))
    c_lf = jnp.pad(jnp.swapaxes(cache_logf_fox, 2, 3), ((0, 0), (0, 0), (0, 8 - N_HEADS), (0, 0)))

    def heads16(a):
        return jnp.pad(a.reshape(n_dec, N_HEADS, HEAD_DIM), ((0, 0), (0, HEAD_ROWS - N_HEADS), (0, 0)))

    xp = x_prompt.reshape(t_p, D_MODEL)
    xs = x_sample.reshape(n_dec, D_MODEL)
    stacked = None
    conv_p = []
    st_s = [[] for _ in range(6)]
    for l in range(depth):
        final = l == depth - 1
        outs = _inproj_prompt(l, depth, xp, g_mix[l], w_nn[l], w_nt[l], bf_r[l], bft_r[l], cw_r[l],
                              g_c[l], tri_in, place, stacked, seq=seq, tm=tm_in)
        qtf, kaug, vtf16, qs, kts16, vs16, oc, utail = outs[:8]
        stacked = outs[8:]
        of = _fox_prompt(qtf, kaug, vtf16, seq=seq, tq=tq_f, tk=tk_f)
        os_ = _sb_prompt(qs, kts16, vs16, su_sb, seq=seq, tq=tq_s, tk=tk_s)
        xp = _outproj(of, oc, os_, xp, g_f[l], g_s[l], w_out_r[l], tm=tm_out)
        xp = _moe(xp, g_ffn[l], w_r[l], b_r[l], w_gate_r[l], w_up_r[l], w_down_r[l], g_fin,
                  tm=tm_moe, final=final)
        conv_p.append(utail.reshape(n_seq, 8, D_CONV)[:, 8 - (CONV_WIDTH - 1):])
        prev = state_conv[l]
        (qf, kf32, vf32, qs, ks32, vs32, logf, oc, u) = _inproj_sample(
            xs, g_mix[l], w_in_r[l], bf_r[l], cw_r[l], g_c[l], prev[:, 0], prev[:, 1])
        lfn = jnp.pad(logf[:, :N_HEADS], ((0, 0), (0, HEAD_ROWS - N_HEADS)))[:, :, None]
        of, os_ = _decode_attn(l, page_table, heads16(qf), heads16(kf32), heads16(vf32), lfn,
                               heads16(qs), su_page, ck_f, cv_f, c_lf, ck_s, cv_s)
        of = of[:, :N_HEADS].reshape(n_dec, D_ATT)
        os_ = os_[:, :N_HEADS].reshape(n_dec, D_ATT)
        xs = _outproj(of, oc, os_, xs, g_f[l], g_s[l], w_out_r[l], tm=n_dec)
        xs = _moe(xs, g_ffn[l], w_r[l], b_r[l], w_gate_r[l], w_up_r[l], w_down_r[l], g_fin,
                  tm=n_dec, final=final)
        hd1 = lambda a: a.reshape(n_dec, 1, N_HEADS, HEAD_DIM)
        new_s = (hd1(kf32), hd1(vf32), logf[:, :N_HEADS].reshape(n_dec, 1, N_HEADS), hd1(ks32), hd1(vs32),
                 jnp.stack([prev[:, 1], u], axis=1))
        for acc, a in zip(st_s, new_s):
            acc.append(a)

    kt_f, vt_f, kt_s, vt_s, lf_t = stacked
    from_lanes = lambda a: jnp.transpose(a.reshape(depth, n_seq, N_HEADS, HEAD_DIM, seq), (0, 1, 4, 2, 3))
    logf_p = jnp.transpose(lf_t[:, :N_HEADS].reshape(depth, N_HEADS, n_seq, seq), (0, 2, 3, 1))
    y_prompt = xp.reshape(n_seq, seq, D_MODEL)
    y_sample = xs.reshape(n_dec, 1, D_MODEL)
    return ((y_prompt, y_sample, from_lanes(kt_f), from_lanes(vt_f), logf_p, from_lanes(kt_s),
             from_lanes(vt_s), jnp.stack(conv_p)) + tuple(jnp.stack(a) for a in st_s))
```

```python
import functools
import math

import jax
import jax.numpy as jnp
from jax import lax
from jax.experimental import pallas as pl
from jax.experimental.pallas import tpu as pltpu

D_MODEL = 1024
HEAD_DIM = 64
N_HEADS = 6
N_PAIRS = N_HEADS // 2
D_ATT = N_HEADS * HEAD_DIM
D_CONV = 256
CONV_WIDTH = 3
N_GROUPS = 4
EXPERTS_PER_GROUP = 8
N_EXPERTS = N_GROUPS * EXPERTS_PER_GROUP
D_EXPERT = 128
D_GROUP = EXPERTS_PER_GROUP * D_EXPERT
PAGE_SIZE = 128
EPS = 1e-6
SCALE = HEAD_DIM ** -0.5
LOG2E = math.log2(math.e)
NEG = -1e30
SIGN_BIT = -2 ** 31
UNDERFLOW_LOG2 = -160.0
UNDERFLOW_LN = UNDERFLOW_LOG2 / LOG2E

LANES = 128
HEAD_ROWS = 16
D_SLOT = N_HEADS * LANES
PAGES_PER_STEP = 16
VMEM_LIMIT = 56 * 1024 * 1024

C_QF, C_KF, C_VF = 0, 384, 768
C_GB, C_GC, C_HC = 1152, 1408, 1664
C_QS, C_KS, C_VS = 1920, 2304, 2688
C_FL = 3072
N_IN_PAD = 3200
P_KA, P_CV, P_QS, P_VS, P_FL, P_END = 0, 768, 1536, 1920, 2304, 2432
T_QF, T_KF, T_VF, T_KS, T_VS, T_FL, T_END = 0, 384, 768, 1152, 1536, 1920, 1936

F32 = jnp.float32
BF16 = jnp.bfloat16


def _rms(x):
    return x * lax.rsqrt(jnp.sum(x * x, axis=-1, keepdims=True) * (1.0 / x.shape[-1]) + EPS)


def _softplus(z):
    return jnp.maximum(z, 0.0) + jnp.log(1.0 + jnp.exp(-jnp.abs(z)))


def _split3(x):
    hi = x.astype(BF16)
    r = x - hi.astype(F32)
    mid = r.astype(BF16)
    lo = (r - mid.astype(F32)).astype(BF16)
    return hi, mid, lo


def _split2(x):
    hi = x.astype(BF16)
    return hi, (x - hi.astype(F32)).astype(BF16)


def _dot(a, b):
    return jnp.dot(a, b, preferred_element_type=F32)


def _dot_nt(a, b):
    return lax.dot_general(a, b, (((1,), (1,)), ((), ())), preferred_element_type=F32)


def _params(*sem):
    return pltpu.CompilerParams(dimension_semantics=sem, vmem_limit_bytes=VMEM_LIMIT)


def _full(shape):
    return pl.BlockSpec(shape, lambda *_: (0,) * len(shape))


def _inproj_prompt_kernel(*refs, tm, tiles_per_seq, n_alias):
    (x_ref, gn_ref, wnn_ref, wnt_ref, bf_ref, bft_ref, cw_ref, gc_ref, tri_ref,
     place_ref) = refs[:10]
    (qtf_ref, kaug_ref, vtf16_ref, qs_ref, kts16_ref, vs16_ref, oc_ref, utail_ref,
     ktf32_ref, vtf32_ref, kts32_ref, vts32_ref, lft_ref, ubuf, fcarry) = refs[10 + n_alias:]
    i = pl.program_id(0)
    xb = (_rms(x_ref[...]) * gn_ref[...]).astype(BF16)

    def nn(a, b):
        return _dot(xb, wnn_ref[:, a:b])

    def nt(a, b):
        return _dot_nt(wnt_ref[a:b, :], xb)

    @pl.when(i % tiles_per_seq == 0)
    def _():
        ubuf[0:8, :] = jnp.zeros((8, D_CONV), F32)
        fcarry[...] = jnp.zeros((1, LANES), F32)

    fl = nn(P_FL, P_END) + bf_ref[...]
    lane = lax.broadcasted_iota(jnp.int32, fl.shape, 1)
    logf = jnp.where(lane < N_HEADS, -_softplus(-fl), 0.0)
    tri = tri_ref[...]
    hi, mid, lo = _split3(logf)
    fcum = _dot(tri, hi) + _dot(tri, mid) + _dot(tri, lo) + fcarry[...]
    fcarry[...] = fcum[tm - 1:tm, :]
    lft_ref[...] = -_softplus(-(nt(T_FL, T_END) + bft_ref[...]))[0:8, :]

    hi, mid, lo = _split3(fcum * (-LOG2E))
    kaug = (nn(P_KA, P_CV) + _dot(hi, place_ref[0]) + _dot(mid, place_ref[1])
            + _dot(lo, place_ref[2]))
    kaug_ref[...] = kaug.astype(BF16)
    qt = (nt(T_QF, T_KF) * (SCALE * LOG2E)).astype(BF16)
    row = lax.broadcasted_iota(jnp.int32, (HEAD_DIM, tm), 0)
    ones_rows = jnp.where(row < 3, 1.0, 0.0).astype(BF16)
    for h in range(N_HEADS):
        qtf_ref[h * LANES:h * LANES + HEAD_DIM, :] = qt[h * HEAD_DIM:(h + 1) * HEAD_DIM, :]
        qtf_ref[h * LANES + HEAD_DIM:(h + 1) * LANES, :] = ones_rows
    ktf32_ref[...] = nt(T_KF, T_VF)
    vt = nt(T_VF, T_KS)
    vtf32_ref[...] = vt
    vtf16_ref[...] = vt.astype(BF16)

    qs_ref[...] = (nn(P_QS, P_VS) * (SCALE * LOG2E)).astype(BF16)
    vs16_ref[...] = nn(P_VS, P_FL).astype(BF16)
    kt = nt(T_KS, T_VS)
    kts32_ref[...] = kt
    kts16_ref[...] = kt.astype(BF16)
    vts32_ref[...] = nt(T_VS, T_FL)

    gch = nn(P_CV, P_QS)
    u = gch[:, D_CONV:2 * D_CONV] * gch[:, 2 * D_CONV:]
    ubuf[8:tm + 8, :] = u
    y = (ubuf[6:tm + 6, :] * cw_ref[0:1, :] + ubuf[7:tm + 7, :] * cw_ref[1:2, :]
         + u * cw_ref[2:3, :])
    tail = u[tm - 8:tm, :]
    ubuf[0:8, :] = tail
    utail_ref[...] = tail
    oc_ref[...] = (_rms(gch[:, :D_CONV] * y) * gc_ref[...]).astype(BF16)


def _inproj_prompt(layer, depth, x, gn, wnn, wnt, bf, bft, cw, gc, tri, place, stacked, *, seq, tm):
    t = x.shape[0]
    n_seq = t // seq
    tps = seq // tm
    row = lambda width: pl.BlockSpec((tm, width), lambda i: (i, 0))
    colt = lambda rows: pl.BlockSpec((rows, tm), lambda i: (0, i))
    seqt = lambda rows: pl.BlockSpec((None, rows, tm), lambda i: (i // tps, 0, i % tps))
    stk = lambda rows: pl.BlockSpec((None, None, rows, tm), lambda i: (layer, i // tps, 0, i % tps))
    stacked_shapes = [jax.ShapeDtypeStruct((depth, n_seq, D_ATT, seq), F32)] * 4 + [
        jax.ShapeDtypeStruct((depth, 8, t), F32)]
    stacked_specs = [stk(D_ATT)] * 4 + [pl.BlockSpec((None, 8, tm), lambda i: (layer, 0, i))]
    out_shape = [jax.ShapeDtypeStruct((D_SLOT, t), BF16), jax.ShapeDtypeStruct((t, D_SLOT), BF16),
                 jax.ShapeDtypeStruct((n_seq, D_ATT, seq), BF16), jax.ShapeDtypeStruct((t, D_ATT), BF16),
                 jax.ShapeDtypeStruct((n_seq, D_ATT, seq), BF16), jax.ShapeDtypeStruct((t, D_ATT), BF16),
                 jax.ShapeDtypeStruct((t, D_CONV), BF16),
                 jax.ShapeDtypeStruct((n_seq * 8, D_CONV), F32)] + stacked_shapes
    out_specs = [colt(D_SLOT), row(D_SLOT), seqt(D_ATT), row(D_ATT), seqt(D_ATT), row(D_ATT),
                 row(D_CONV), pl.BlockSpec((8, D_CONV), lambda i: (i // tps, 0))] + stacked_specs
    in_specs = [row(D_MODEL), _full((1, D_MODEL)), _full((D_MODEL, P_END)), _full((T_END, D_MODEL)),
                _full((1, LANES)), _full((HEAD_ROWS, 1)), _full((8, D_CONV)), _full((1, D_CONV)),
                _full((tm, tm)), _full((3, LANES, D_SLOT))]
    args = [x, gn, wnn, wnt, bf, bft, cw, gc, tri, place]
    aliases = {}
    n_alias = 0
    if stacked is not None:
        n_alias = len(stacked)
        in_specs += [pl.BlockSpec(memory_space=pl.ANY)] * n_alias
        aliases = {len(args) + k: 8 + k for k in range(n_alias)}
        args += list(stacked)
    return pl.pallas_call(
        functools.partial(_inproj_prompt_kernel, tm=tm, tiles_per_seq=tps, n_alias=n_alias),
        grid=(t // tm,), in_specs=in_specs, out_specs=out_specs, out_shape=out_shape,
        input_output_aliases=aliases,
        scratch_shapes=[pltpu.VMEM((tm + 8, D_CONV), F32), pltpu.VMEM((1, LANES), F32)],
        compiler_params=_params("arbitrary"), name="inproj_prompt",
    )(*args)


def _inproj_sample_kernel(x_ref, gn_ref, w_ref, bf_ref, cw_ref, gc_ref, p0_ref, p1_ref,
                          qf_ref, kf32_ref, vf32_ref, qs_ref, ks32_ref, vs32_ref,
                          logf_ref, oc_ref, u_ref):
    xb = (_rms(x_ref[...]) * gn_ref[...]).astype(BF16)

    def proj(a, b):
        return _dot(xb, w_ref[:, a:b])

    for c_q, q_ref, k_ref, v_ref in ((C_QF, qf_ref, kf32_ref, vf32_ref),
                                     (C_QS, qs_ref, ks32_ref, vs32_ref)):
        qkv = proj(c_q, c_q + 3 * D_ATT)
        q_ref[...] = qkv[:, :D_ATT] * SCALE
        k_ref[...] = qkv[:, D_ATT:2 * D_ATT]
        v_ref[...] = qkv[:, 2 * D_ATT:]
    fl = proj(C_FL, N_IN_PAD) + bf_ref[...]
    lane = lax.broadcasted_iota(jnp.int32, fl.shape, 1)
    logf_ref[...] = jnp.where(lane < N_HEADS, -_softplus(-fl), 0.0)
    gch = proj(C_GB, C_GB + 3 * D_CONV)
    u = gch[:, D_CONV:2 * D_CONV] * gch[:, 2 * D_CONV:]
    u_ref[...] = u
    y = p0_ref[...] * cw_ref[0:1, :] + p1_ref[...] * cw_ref[1:2, :] + u * cw_ref[2:3, :]
    oc_ref[...] = (_rms(gch[:, :D_CONV] * y) * gc_ref[...]).astype(BF16)


def _inproj_sample(x, gn, w, bf, cw, gc, p0, p1):
    t = x.shape[0]
    att = jax.ShapeDtypeStruct((t, D_ATT), F32)
    out_shape = [att] * 6 + [jax.ShapeDtypeStruct((t, LANES), F32),
                             jax.ShapeDtypeStruct((t, D_CONV), BF16),
                             jax.ShapeDtypeStruct((t, D_CONV), F32)]
    return pl.pallas_call(
        _inproj_sample_kernel, out_shape=out_shape,
        compiler_params=pltpu.CompilerParams(vmem_limit_bytes=VMEM_LIMIT), name="inproj_sample",
    )(x, gn, w, bf, cw, gc, p0, p1)


def _fox_prompt_kernel(qt_ref, k_ref, vt_ref, o_ref, sa_s, sb_s, m_s, l_s, acc_s, *, tq, tk):
    qi = pl.program_id(2)
    m_s[...] = jnp.full(m_s.shape, NEG, F32)
    l_s[...] = jnp.zeros(l_s.shape, F32)
    acc_s[...] = jnp.zeros(acc_s.shape, F32)

    def scores(j, s_ref):
        start = pl.multiple_of(j * tk, tk)
        for h in range(2):
            k = k_ref[pl.ds(start, tk), h * LANES:(h + 1) * LANES]
            s_ref[h] = _dot(k, qt_ref[h * LANES:(h + 1) * LANES, :])

    def update(j, s_ref, masked):
        start = pl.multiple_of(j * tk, tk)
        if masked:
            key = lax.broadcasted_iota(jnp.int32, (tk, tq), 0) + start
            qry = lax.broadcasted_iota(jnp.int32, (tk, tq), 1) + qi * tq
            causal = key <= qry
        for h in range(2):
            s = s_ref[h]
            if masked:
                s = jnp.where(causal, s, NEG)
            m_old = m_s[h]
            m_new = jnp.maximum(m_old, jnp.max(s, axis=0, keepdims=True))
            alpha = jnp.exp2(m_old - m_new)
            p = jnp.exp2(s - m_new)
            l_s[h] = alpha * l_s[h] + jnp.sum(p, axis=0, keepdims=True)
            vt = vt_ref[h * HEAD_DIM:(h + 1) * HEAD_DIM, pl.ds(start, tk)]
            acc_s[h] = alpha * acc_s[h] + _dot(vt, p.astype(BF16))
            m_s[h] = m_new

    assert tq == 2 * tk
    scores(0, sa_s)

    def pair(i):
        scores(2 * i + 1, sb_s)
        update(2 * i, sa_s, False)
        scores(2 * i + 2, sa_s)
        update(2 * i + 1, sb_s, False)

    def two_pairs(i, carry):
        pair(2 * i)
        pair(2 * i + 1)
        return carry

    lax.fori_loop(0, qi // 2, two_pairs, 0)

    @pl.when(qi % 2 == 1)
    def _():
        pair(qi - 1)

    scores(2 * qi + 1, sb_s)
    update(2 * qi, sa_s, True)
    update(2 * qi + 1, sb_s, True)

    o_t = jnp.concatenate([acc_s[0] / l_s[0], acc_s[1] / l_s[1]], axis=0)
    o_ref[...] = o_t.T


def _split_heads(q_ref, q_s):
    q = q_ref[...]
    lo = lax.broadcasted_iota(jnp.int32, q.shape, 1) < HEAD_DIM
    zero = jnp.zeros_like(q)
    q_s[0] = jnp.where(lo, q, zero)
    q_s[1] = jnp.where(lo, zero, q)


def _sb_prompt_kernel(q_ref, kt_ref, v_ref, su_ref, o_ref, q_s, za_s, zb_s, r_s, acc_s, *, tq, tk):
    qi = pl.program_id(2)
    _split_heads(q_ref, q_s)
    r_s[...] = jnp.zeros(r_s.shape, F32)
    acc_s[...] = jnp.zeros(acc_s.shape, F32)

    def logits(j, z_ref):
        start = pl.multiple_of(jnp.maximum(j, 0) * tk, tk)
        kt = kt_ref[:, pl.ds(start, tk)]
        for h in range(2):
            z_ref[h] = _dot(q_s[h], kt)

    def update(j, z_ref, masked):
        start = pl.multiple_of(j * tk, tk)
        v = v_ref[pl.ds(start, tk), :]
        su = su_ref[...]
        if masked:
            row = lax.broadcasted_iota(jnp.int32, (tq, tk), 0) + qi * tq
            col = lax.broadcasted_iota(jnp.int32, (tq, tk), 1) + start
            strict = col < row
        exponents = []
        for h in range(2):
            z = z_ref[h]
            neg_abs = pltpu.bitcast(pltpu.bitcast(z, jnp.int32) | SIGN_BIT, F32)
            ls = jnp.minimum(z, 0.0) - jnp.log2(1.0 + jnp.exp2(neg_abs))
            lk = ls - z
            if masked:
                lk = jnp.where(strict, lk, 0.0)
            r_old = r_s[h]
            after = _dot(lk.astype(BF16), su) + jnp.concatenate([r_old] * (tk // LANES), axis=1)
            exponents.append(ls + after)
            r_s[h] = r_old + jnp.sum(lk, axis=-1, keepdims=True)
        for h in range(2):
            a = jnp.exp2(exponents[h])
            if masked:
                a = jnp.where(strict, a, 0.0)
            acc_s[h] = acc_s[h] + _dot(a.astype(BF16), v)

    assert tq == 2 * tk
    top = 2 * qi + 1
    logits(top, za_s)
    logits(top - 1, zb_s)
    update(top, za_s, True)
    logits(top - 2, za_s)
    update(top - 1, zb_s, True)

    def pair(j):
        logits(j - 1, zb_s)
        update(j, za_s, False)
        logits(j - 2, za_s)
        update(j - 1, zb_s, False)

    def some_weight_nonzero():
        r_max = jnp.max(jnp.maximum(r_s[0], r_s[1]), axis=0, keepdims=True)
        return r_max[0, 0] > UNDERFLOW_LOG2

    def more(c):
        i, alive = c
        return jnp.logical_and(i < qi, alive)

    def step(c):
        i, _ = c
        pair(top - 2 - 2 * i)
        return i + 1, some_weight_nonzero()

    lax.while_loop(more, step, (jnp.int32(0), some_weight_nonzero()))

    lo = lax.broadcasted_iota(jnp.int32, (tq, LANES), 1) < HEAD_DIM
    o_ref[...] = jnp.where(lo, acc_s[0], acc_s[1])


def _fox_prompt(qt, kaug, vt, *, seq, tq, tk):
    t = kaug.shape[0]
    n_seq, nq = t // seq, seq // tq
    o_spec = pl.BlockSpec((tq, LANES), lambda b, p, i: (b * nq + i, p))
    return pl.pallas_call(
        functools.partial(_fox_prompt_kernel, tq=tq, tk=tk), grid=(n_seq, N_PAIRS, nq),
        in_specs=[pl.BlockSpec((2 * LANES, tq), lambda b, p, i: (p, b * nq + i)),
                  pl.BlockSpec((seq, 2 * LANES), lambda b, p, i: (b, p)),
                  pl.BlockSpec((None, LANES, seq), lambda b, p, i: (b, p, 0))],
        out_specs=o_spec, out_shape=jax.ShapeDtypeStruct((t, D_ATT), F32),
        scratch_shapes=[pltpu.VMEM((2, tk, tq), F32), pltpu.VMEM((2, tk, tq), F32),
                        pltpu.VMEM((2, 1, tq), F32), pltpu.VMEM((2, 1, tq), F32),
                        pltpu.VMEM((2, HEAD_DIM, tq), F32)],
        compiler_params=_params("arbitrary", "arbitrary", "arbitrary"), name="fox_prompt",
    )(qt, kaug, vt)


def _sb_prompt(q, kt, v, su, *, seq, tq, tk):
    t = q.shape[0]
    n_seq, nq = t // seq, seq // tq
    q_spec = pl.BlockSpec((tq, LANES), lambda b, p, i: (b * nq + i, p))
    return pl.pallas_call(
        functools.partial(_sb_prompt_kernel, tq=tq, tk=tk), grid=(n_seq, N_PAIRS, nq),
        in_specs=[q_spec,
                  pl.BlockSpec((None, LANES, seq), lambda b, p, i: (b, p, 0)),
                  pl.BlockSpec((seq, LANES), lambda b, p, i: (b, p)),
                  pl.BlockSpec((tk, tk), lambda b, p, i: (0, 0))],
        out_specs=q_spec, out_shape=jax.ShapeDtypeStruct((t, D_ATT), F32),
        scratch_shapes=[pltpu.VMEM((2, tq, LANES), BF16), pltpu.VMEM((2, tq, tk), F32),
                        pltpu.VMEM((2, tq, tk), F32), pltpu.VMEM((2, tq, LANES), F32),
                        pltpu.VMEM((2, tq, LANES), F32)],
        compiler_params=_params("arbitrary", "arbitrary", "arbitrary"), name="sb_prompt",
    )(q, kt, v, su)


def _pick_rows(parts):
    rows = lax.broadcasted_iota(jnp.int32, parts[0].shape, 0)
    out = jnp.zeros_like(parts[0])
    for h, part in enumerate(parts):
        out = jnp.where(rows == h, part, out)
    return out


def _suffix_over_pages(x, su, carry):
    g_n = x.shape[1] // PAGE_SIZE
    stacked = jnp.concatenate([x[:, g * PAGE_SIZE:(g + 1) * PAGE_SIZE] for g in range(g_n)], axis=0)
    hi, mid, lo = _split3(stacked)
    inside = _dot(hi, su) + _dot(mid, su) + _dot(lo, su)
    total = jnp.sum(stacked, axis=-1, keepdims=True)
    outs = []
    for g in range(g_n):
        outs.append(inside[g * HEAD_ROWS:(g + 1) * HEAD_ROWS] + carry)
        carry = carry + total[g * HEAD_ROWS:(g + 1) * HEAD_ROWS]
    return jnp.concatenate(outs, axis=1), carry


def _decode_attn_kernel(pt_ref, qf_ref, knf_ref, vnf_ref, lfn_ref, qs_ref, su_ref, *refs):
    del pt_ref
    g_n = PAGES_PER_STEP
    kf, vf, lf, ks, vs = (refs[k * g_n:(k + 1) * g_n] for k in range(5))
    of_ref, os_ref, m_s, l_s, g_s, accf_s, r_s, accs_s = refs[5 * g_n:]
    p = pl.program_id(1)

    @pl.when(p == 0)
    def _():
        m_s[...] = jnp.full(m_s.shape, NEG, F32)
        l_s[...] = jnp.zeros(l_s.shape, F32)
        g_s[...] = jnp.zeros(g_s.shape, F32)
        r_s[...] = jnp.zeros(r_s.shape, F32)
        accf_s[...] = jnp.zeros(accf_s.shape, F32)
        accs_s[...] = jnp.zeros(accs_s.shape, F32)

    su = su_ref[...]
    qf = qf_ref[...]

    def wide(page_refs, h):
        return jnp.concatenate([r[h].astype(BF16) for r in page_refs], axis=1)

    def scores(q, page_refs):
        qb = q.astype(BF16)
        return _pick_rows([_dot(qb, wide(page_refs, h)) for h in range(N_HEADS)])

    zeros = jnp.zeros((HEAD_ROWS - 8, PAGE_SIZE), F32)
    lf_w = jnp.concatenate([jnp.concatenate([r[...], zeros], axis=0) for r in lf], axis=1)
    decay, g_new = _suffix_over_pages(lf_w, su, g_s[...])
    g_s[...] = g_new
    s = scores(qf, kf) + decay + lfn_ref[...]
    m_old = m_s[...]
    m_new = jnp.maximum(m_old, jnp.max(s, axis=-1, keepdims=True))
    alpha = jnp.exp(m_old - m_new)
    pr = jnp.exp(s - m_new)
    l_s[...] = alpha * l_s[...] + jnp.sum(pr, axis=-1, keepdims=True)
    m_s[...] = m_new
    prb = pr.astype(BF16)
    for h in range(N_HEADS):
        accf_s[h] = alpha * accf_s[h] + _dot_nt(prb, wide(vf, h))

    @pl.when(jnp.max(r_s[...], axis=0, keepdims=True)[0, 0] > UNDERFLOW_LN)
    def _():
        z = scores(qs_ref[...], ks)
        lk = -_softplus(z)
        after, r_new = _suffix_over_pages(lk, su, r_s[...])
        r_s[...] = r_new
        ab = jnp.exp(z + lk + after).astype(BF16)
        for h in range(N_HEADS):
            accs_s[h] = accs_s[h] + _dot_nt(ab, wide(vs, h))

    @pl.when(p == pl.num_programs(1) - 1)
    def _():
        s_self = jnp.sum(qf * knf_ref[...], axis=-1, keepdims=True)
        m_old = m_s[...]
        m_new = jnp.maximum(m_old, s_self)
        alpha = jnp.exp(m_old - m_new)
        p_self = jnp.exp(s_self - m_new)
        inv_l = 1.0 / (alpha * l_s[...] + p_self)
        own = p_self * vnf_ref[...]
        of_ref[...] = _pick_rows([alpha * accf_s[h] + own for h in range(N_HEADS)]) * inv_l
        os_ref[...] = _pick_rows([accs_s[h] for h in range(N_HEADS)])


def _decode_attn(layer, page_table, qf, knf, vnf, lfn, qs, su,
                 cache_kf, cache_vf, cache_lf, cache_ks, cache_vs):
    n_b, n_pages = page_table.shape
    g_n = PAGES_PER_STEP
    per_b = lambda shape: pl.BlockSpec((None,) + shape, lambda b, p, pt: (b,) + (0,) * len(shape))

    def pages(shape):
        return [pl.BlockSpec((None, None) + shape,
                             lambda b, p, pt, g=g: (layer, pt[b, n_pages - 1 - (p * g_n + g)])
                             + (0,) * len(shape)) for g in range(g_n)]

    head_blk = per_b((HEAD_ROWS, HEAD_DIM))
    kv_pages = pages((N_HEADS, HEAD_DIM, PAGE_SIZE))
    out = jax.ShapeDtypeStruct((n_b, HEAD_ROWS, HEAD_DIM), F32)
    col = pltpu.VMEM((HEAD_ROWS, 1), F32)
    acc = pltpu.VMEM((N_HEADS, HEAD_ROWS, HEAD_DIM), F32)
    return pl.pallas_call(
        _decode_attn_kernel,
        grid_spec=pltpu.PrefetchScalarGridSpec(
            num_scalar_prefetch=1, grid=(n_b, n_pages // g_n),
            in_specs=[head_blk, head_blk, head_blk, per_b((HEAD_ROWS, 1)), head_blk,
                      pl.BlockSpec((PAGE_SIZE, PAGE_SIZE), lambda b, p, pt: (0, 0))]
            + kv_pages + kv_pages + pages((8, PAGE_SIZE)) + kv_pages + kv_pages,
            out_specs=[head_blk, head_blk],
            scratch_shapes=[col, col, col, acc, col, acc]),
        out_shape=[out, out],
        compiler_params=_params("arbitrary", "arbitrary"), name="decode_attn",
    )(page_table, qf, knf, vnf, lfn, qs, su, *([cache_kf] * g_n), *([cache_vf] * g_n),
      *([cache_lf] * g_n), *([cache_ks] * g_n), *([cache_vs] * g_n))


def _outproj_kernel(of_ref, oc_ref, os_ref, x_ref, gf_ref, gs_ref, w_ref, o_ref):
    nf = (_rms(of_ref[...]) * gf_ref[...]).astype(BF16)
    ns = (_rms(os_ref[...]) * gs_ref[...]).astype(BF16)
    y = (_dot(nf, w_ref[0:D_ATT, :]) + _dot(oc_ref[...], w_ref[D_ATT:D_ATT + D_CONV, :])
         + _dot(ns, w_ref[D_ATT + D_CONV:, :]))
    o_ref[...] = x_ref[...] + y


def _outproj(of, oc, os_, x, gf, gs, w, *, tm):
    t = x.shape[0]
    row = lambda width: pl.BlockSpec((tm, width), lambda i: (i, 0))
    return pl.pallas_call(
        _outproj_kernel, grid=(t // tm,),
        in_specs=[row(D_ATT), row(D_CONV), row(D_ATT), row(D_MODEL), _full((1, D_ATT)),
                  _full((1, D_ATT)), _full((D_MODEL, D_MODEL))],
        out_specs=row(D_MODEL), out_shape=jax.ShapeDtypeStruct((t, D_MODEL), F32),
        compiler_params=_params("arbitrary"), name="outproj",
    )(of, oc, os_, x, gf, gs, w)


def _route(logits):
    lane_i = lax.broadcasted_iota(jnp.int32, logits.shape, 1)
    lane = lane_i.astype(F32)
    lane_group = jnp.right_shift(lane_i, EXPERTS_PER_GROUP.bit_length() - 1).astype(F32)
    big = float(LANES)
    is_group = (lane_i >= N_EXPERTS) & (lane_i < N_EXPERTS + N_GROUPS)
    gl = jnp.where(is_group, logits, NEG)
    g_max = jnp.max(gl, axis=-1, keepdims=True)
    g_idx = jnp.min(jnp.where(gl == g_max, lane, big), axis=-1, keepdims=True) - N_EXPERTS
    p_group = 1.0 / jnp.sum(jnp.exp(gl - g_max), axis=-1, keepdims=True)
    in_group = (lane_i < N_EXPERTS) & (lane_group == g_idx)
    el = jnp.where(in_group, logits, NEG)
    m1 = jnp.max(el, axis=-1, keepdims=True)
    i1 = jnp.min(jnp.where(el == m1, lane, big), axis=-1, keepdims=True)
    el2 = jnp.where(lane == i1, NEG, el)
    m2 = jnp.max(el2, axis=-1, keepdims=True)
    i2 = jnp.min(jnp.where(el2 == m2, lane, big), axis=-1, keepdims=True)
    e21 = jnp.exp(m2 - m1)
    w1 = p_group / (1.0 + e21)
    return jnp.where(lane == i1, w1, 0.0) + jnp.where(lane == i2, w1 * e21, 0.0)


def _moe_kernel(x_ref, gn_ref, wr_ref, br_ref, wg_ref, wu_ref, wd_ref, gfin_ref, o_ref,
                xn_s, gate_s, *, final):
    g = pl.program_id(1)

    @pl.when(g == 0)
    def _():
        x = x_ref[...]
        xn = _rms(x) * gn_ref[...]
        xn_s[...] = xn.astype(BF16)
        logits = jnp.dot(xn, wr_ref[...], preferred_element_type=F32,
                         precision=lax.Precision.HIGHEST) + br_ref[...]
        gate = _route(logits)
        for gi in range(N_GROUPS):
            shift = (LANES - gi * EXPERTS_PER_GROUP) % LANES
            gate_s[gi] = pltpu.roll(gate, shift, 1) if shift else gate
        o_ref[...] = x

    xb = xn_s[...]
    hg = _dot(xb, wg_ref[...])
    hu = _dot(xb, wu_ref[...])
    h = hg * (1.0 / (1.0 + jnp.exp(-hg))) * hu
    gate = gate_s[g]
    hgated = jnp.concatenate(
        [h[:, e * D_EXPERT:(e + 1) * D_EXPERT] * gate[:, e:e + 1] for e in range(EXPERTS_PER_GROUP)],
        axis=1).astype(BF16)
    o_ref[...] += _dot(hgated, wd_ref[...])

    if final:
        @pl.when(g == N_GROUPS - 1)
        def _():
            o_ref[...] = _rms(o_ref[...]) * gfin_ref[...]


def _moe(x, gn, wr, br, wg, wu, wd, gfin, *, tm, final):
    t = x.shape[0]
    row = pl.BlockSpec((tm, D_MODEL), lambda i, g: (i, 0))
    const = lambda shape: pl.BlockSpec(shape, lambda i, g: (0,) * len(shape))
    return pl.pallas_call(
        functools.partial(_moe_kernel, final=final), grid=(t // tm, N_GROUPS),
        in_specs=[row, const((1, D_MODEL)), const((D_MODEL, LANES)), const((1, LANES)),
                  pl.BlockSpec((D_MODEL, D_GROUP), lambda i, g: (0, g)),
                  pl.BlockSpec((D_MODEL, D_GROUP), lambda i, g: (0, g)),
                  pl.BlockSpec((D_GROUP, D_MODEL), lambda i, g: (g, 0)),
                  const((1, D_MODEL))],
        out_specs=row, out_shape=jax.ShapeDtypeStruct((t, D_MODEL), F32),
        scratch_shapes=[pltpu.VMEM((tm, D_MODEL), BF16), pltpu.VMEM((N_GROUPS, tm, LANES), F32)],
        compiler_params=_params("arbitrary", "arbitrary"), name="moe",
    )(x, gn, wr, br, wg, wu, wd, gfin)


def _tri(n, rel):
    r = lax.broadcasted_iota(jnp.int32, (n, n), 0)
    c = lax.broadcasted_iota(jnp.int32, (n, n), 1)
    return rel(r, c).astype(BF16)


def kernel(x_prompt, x_sample, cache_k_fox, cache_v_fox, cache_logf_fox, cache_k_sb, cache_v_sb,
           state_conv, page_table, g_mix_norm, w_in, b_f, conv_w, g_mix_out, w_out, g_ffn_norm,
           w_router_g, b_router_g, w_router_e, b_router_e, w_gate, w_up, w_down, g_final):
    depth = w_in.shape[0]
    n_seq, seq, _ = x_prompt.shape
    n_dec = x_sample.shape[0]
    t_p = n_seq * seq
    tm_in, tm_out, tm_moe = 256, 512, 512
    tq_f, tk_f, tq_s, tk_s = 512, 256, 512, 256

    c_fl = 3 * D_ATT
    c_end = c_fl + N_HEADS
    wq_f, wk_f, wv_f = w_in[..., :D_ATT], w_in[..., D_ATT:2 * D_ATT], w_in[..., 2 * D_ATT:c_fl]
    w_f = w_in[..., c_fl:c_end]
    w_cv = w_in[..., c_end:c_end + 3 * D_CONV]
    c_s = c_end + 3 * D_CONV
    wq_s, wk_s, wv_s = (w_in[..., c_s:c_s + D_ATT], w_in[..., c_s + D_ATT:c_s + 2 * D_ATT],
                        w_in[..., c_s + 2 * D_ATT:])
    pad_cols = lambda w, n: jnp.pad(w, ((0, 0), (0, 0), (0, n - w.shape[-1])))
    w_in_r = jnp.concatenate([wq_f, wk_f, wv_f, w_cv, wq_s, wk_s, wv_s, pad_cols(w_f, LANES)],
                             axis=-1).astype(BF16)
    wk_slots = jnp.pad(wk_f.reshape(depth, D_MODEL, N_HEADS, HEAD_DIM),
                       ((0, 0), (0, 0), (0, 0), (0, LANES - HEAD_DIM))).reshape(depth, D_MODEL, D_SLOT)
    w_nn = jnp.concatenate([wk_slots, w_cv, wq_s, wv_s, pad_cols(w_f, LANES)], axis=-1).astype(BF16)
    w_nt = jnp.swapaxes(jnp.concatenate([wq_f, wk_f, wv_f, wk_s, wv_s, pad_cols(w_f, HEAD_ROWS)],
                                        axis=-1), 1, 2).astype(BF16)
    bf_r = jnp.pad(b_f, ((0, 0), (0, LANES - N_HEADS)))[:, None, :]
    bft_r = jnp.pad(b_f, ((0, 0), (0, HEAD_ROWS - N_HEADS)))[:, :, None]
    cw_r = jnp.pad(conv_w, ((0, 0), (0, 8 - CONV_WIDTH), (0, 0)))
    g_f = g_mix_out[:, None, :D_ATT]
    g_c = g_mix_out[:, None, D_ATT:D_ATT + D_CONV]
    g_s = g_mix_out[:, None, D_ATT + D_CONV:]
    w_out_r = w_out.astype(BF16)
    w_r = jnp.concatenate(
        [w_router_e, w_router_g,
         jnp.zeros((depth, D_MODEL, LANES - N_EXPERTS - N_GROUPS), F32)], axis=-1)
    b_r = jnp.concatenate(
        [b_router_e, b_router_g, jnp.zeros((depth, LANES - N_EXPERTS - N_GROUPS), F32)],
        axis=-1)[:, None, :]
    w_gate_r = jnp.transpose(w_gate, (0, 2, 1, 3)).reshape(depth, D_MODEL, -1).astype(BF16)
    w_up_r = jnp.transpose(w_up, (0, 2, 1, 3)).reshape(depth, D_MODEL, -1).astype(BF16)
    w_down_r = w_down.reshape(depth, -1, D_MODEL).astype(BF16)
    g_mix = g_mix_norm[:, None, :]
    g_ffn = g_ffn_norm[:, None, :]
    g_fin = g_final[None, :]

    tri_in = _tri(tm_in, lambda r, c: c <= r)
    su_sb = _tri(tk_s, lambda r, c: r > c)
    su_page = _tri(PAGE_SIZE, lambda r, c: r > c)
    pr = lax.broadcasted_iota(jnp.int32, (3, LANES, D_SLOT), 1)
    pc = lax.broadcasted_iota(jnp.int32, (3, LANES, D_SLOT), 2)
    pi = lax.broadcasted_iota(jnp.int32, (3, LANES, D_SLOT), 0)
    place = ((pr < N_HEADS) & (pc == pr * LANES + HEAD_DIM + pi)).astype(BF16)

    to_lanes = lambda c: jnp.transpose(c, (0, 1, 3, 4, 2))
    ck_f, cv_f, ck_s, cv_s = map(to_lanes, (cache_k_fox, cache_v_fox, cache_k_sb, cache_v_sb))
    c_lf = jnp.pad(jnp.swapaxes(cache_logf_fox, 2, 3), ((0, 0), (0, 0), (0, 8 - N_HEADS), (0, 0)))

    def heads16(a):
        return jnp.pad(a.reshape(n_dec, N_HEADS, HEAD_DIM), ((0, 0), (0, HEAD_ROWS - N_HEADS), (0, 0)))

    xp = x_prompt.reshape(t_p, D_MODEL)
    xs = x_sample.reshape(n_dec, D_MODEL)
    stacked = None
    conv_p = []
    st_s = [[] for _ in range(6)]
    for l in range(depth):
        final = l == depth - 1
        outs = _inproj_prompt(l, depth, xp, g_mix[l], w_nn[l], w_nt[l], bf_r[l], bft_r[l], cw_r[l],
                              g_c[l], tri_in, place, stacked, seq=seq, tm=tm_in)
        qtf, kaug, vtf16, qs, kts16, vs16, oc, utail = outs[:8]
        stacked = outs[8:]
        of = _fox_prompt(qtf, kaug, vtf16, seq=seq, tq=tq_f, tk=tk_f)
        os_ = _sb_prompt(qs, kts16, vs16, su_sb, seq=seq, tq=tq_s, tk=tk_s)
        xp = _outproj(of, oc, os_, xp, g_f[l], g_s[l], w_out_r[l], tm=tm_out)
        xp = _moe(xp, g_ffn[l], w_r[l], b_r[l], w_gate_r[l], w_up_r[l], w_down_r[l], g_fin,
                  tm=tm_moe, final=final)
        conv_p.append(utail.reshape(n_seq, 8, D_CONV)[:, 8 - (CONV_WIDTH - 1):])
        prev = state_conv[l]
        (qf, kf32, vf32, qs, ks32, vs32, logf, oc, u) = _inproj_sample(
            xs, g_mix[l], w_in_r[l], bf_r[l], cw_r[l], g_c[l], prev[:, 0], prev[:, 1])
        lfn = jnp.pad(logf[:, :N_HEADS], ((0, 0), (0, HEAD_ROWS - N_HEADS)))[:, :, None]
        of, os_ = _decode_attn(l, page_table, heads16(qf), heads16(kf32), heads16(vf32), lfn,
                               heads16(qs), su_page, ck_f, cv_f, c_lf, ck_s, cv_s)
        of = of[:, :N_HEADS].reshape(n_dec, D_ATT)
        os_ = os_[:, :N_HEADS].reshape(n_dec, D_ATT)
        xs = _outproj(of, oc, os_, xs, g_f[l], g_s[l], w_out_r[l], tm=n_dec)
        xs = _moe(xs, g_ffn[l], w_r[l], b_r[l], w_gate_r[l], w_up_r[l], w_down_r[l], g_fin,
                  tm=n_dec, final=final)
        hd1 = lambda a: a.reshape(n_dec, 1, N_HEADS, HEAD_DIM)
        new_s = (hd1(kf32), hd1(vf32), logf[:, :N_HEADS].reshape(n_dec, 1, N_HEADS), hd1(ks32), hd1(vs32),
                 jnp.stack([prev[:, 1], u], axis=1))
        for acc, a in zip(st_s, new_s):
            acc.append(a)

    kt_f, vt_f, kt_s, vt_s, lf_t = stacked
    from_lanes = lambda a: jnp.transpose(a.reshape(depth, n_seq, N_HEADS, HEAD_DIM, seq), (0, 1, 4, 2, 3))
    logf_p = jnp.transpose(lf_t[:, :N_HEADS].reshape(depth, N_HEADS, n_seq, seq), (0, 2, 3, 1))
    y_prompt = xp.reshape(n_seq, seq, D_MODEL)
    y_sample = xs.reshape(n_dec, 1, D_MODEL)
    return ((y_prompt, y_sample, from_lanes(kt_f), from_lanes(vt_f), logf_p, from_lanes(kt_s),
             from_lanes(vt_s), jnp.stack(conv_p)) + tuple(jnp.stack(a) for a in st_s))
```

```python
import functools
import math

import jax
import jax.numpy as jnp
from jax import lax
from jax.experimental import pallas as pl
from jax.experimental.pallas import tpu as pltpu

D_MODEL = 1024
HEAD_DIM = 64
N_HEADS = 6
N_PAIRS = N_HEADS // 2
D_ATT = N_HEADS * HEAD_DIM
D_CONV = 256
CONV_WIDTH = 3
N_GROUPS = 4
EXPERTS_PER_GROUP = 8
N_EXPERTS = N_GROUPS * EXPERTS_PER_GROUP
D_EXPERT = 128
D_GROUP = EXPERTS_PER_GROUP * D_EXPERT
PAGE_SIZE = 128
EPS = 1e-6
SCALE = HEAD_DIM ** -0.5
LOG2E = math.log2(math.e)
NEG = -1e30
SIGN_BIT = -2 ** 31
UNDERFLOW_LOG2 = -160.0
UNDERFLOW_LN = UNDERFLOW_LOG2 / LOG2E

LANES = 128
HEAD_ROWS = 16
D_SLOT = N_HEADS * LANES
VT_ROWS = HEAD_DIM + HEAD_ROWS
PAGES_PER_STEP = 16
VMEM_LIMIT = 56 * 1024 * 1024

C_QF, C_KF, C_VF = 0, 384, 768
C_GB, C_GC, C_HC = 1152, 1408, 1664
C_QS, C_KS, C_VS = 1920, 2304, 2688
C_FL = 3072
N_IN_PAD = 3200
P_KA, P_CV, P_QS, P_VS, P_FL, P_END = 0, 768, 1536, 1920, 2304, 2432
T_QF, T_KF, T_VF, T_KS, T_VS, T_FL, T_END = 0, 384, 768, 1152, 1536, 1920, 1936

F32 = jnp.float32
BF16 = jnp.bfloat16


def _rms(x):
    return x * lax.rsqrt(jnp.sum(x * x, axis=-1, keepdims=True) * (1.0 / x.shape[-1]) + EPS)


def _softplus(z):
    return jnp.maximum(z, 0.0) + jnp.log(1.0 + jnp.exp(-jnp.abs(z)))


def _split3(x):
    hi = x.astype(BF16)
    r = x - hi.astype(F32)
    mid = r.astype(BF16)
    lo = (r - mid.astype(F32)).astype(BF16)
    return hi, mid, lo


def _split2(x):
    hi = x.astype(BF16)
    return hi, (x - hi.astype(F32)).astype(BF16)


def _dot(a, b):
    return jnp.dot(a, b, preferred_element_type=F32)


def _dot_nt(a, b):
    return lax.dot_general(a, b, (((1,), (1,)), ((), ())), preferred_element_type=F32)


def _params(*sem):
    return pltpu.CompilerParams(dimension_semantics=sem, vmem_limit_bytes=VMEM_LIMIT)


def _full(shape):
    return pl.BlockSpec(shape, lambda *_: (0,) * len(shape))


def _inproj_prompt_kernel(*refs, tm, tiles_per_seq, n_alias):
    (x_ref, gn_ref, wnn_ref, wnt_ref, bf_ref, bft_ref, cw_ref, gc_ref, tri_ref,
     place_ref) = refs[:10]
    (qtf_ref, kaug_ref, vtf16_ref, qs_ref, kts16_ref, vs16_ref, oc_ref, utail_ref,
     ktf32_ref, vtf32_ref, kts32_ref, vts32_ref, lft_ref, ubuf, fcarry) = refs[10 + n_alias:]
    i = pl.program_id(0)
    xb = (_rms(x_ref[...]) * gn_ref[...]).astype(BF16)

    def nn(a, b):
        return _dot(xb, wnn_ref[:, a:b])

    def nt(a, b):
        return _dot_nt(wnt_ref[a:b, :], xb)

    @pl.when(i % tiles_per_seq == 0)
    def _():
        ubuf[0:8, :] = jnp.zeros((8, D_CONV), F32)
        fcarry[...] = jnp.zeros((1, LANES), F32)

    fl = nn(P_FL, P_END) + bf_ref[...]
    lane = lax.broadcasted_iota(jnp.int32, fl.shape, 1)
    logf = jnp.where(lane < N_HEADS, -_softplus(-fl), 0.0)
    tri = tri_ref[...]
    hi, mid, lo = _split3(logf)
    fcum = _dot(tri, hi) + _dot(tri, mid) + _dot(tri, lo) + fcarry[...]
    fcarry[...] = fcum[tm - 1:tm, :]
    lft_ref[...] = -_softplus(-(nt(T_FL, T_END) + bft_ref[...]))[0:8, :]

    hi, mid, lo = _split3(fcum * (-LOG2E))
    kaug = (nn(P_KA, P_CV) + _dot(hi, place_ref[0]) + _dot(mid, place_ref[1])
            + _dot(lo, place_ref[2]))
    kaug_ref[...] = kaug.astype(BF16)
    qt = (nt(T_QF, T_KF) * (SCALE * LOG2E)).astype(BF16)
    row = lax.broadcasted_iota(jnp.int32, (HEAD_DIM, tm), 0)
    ones_rows = jnp.where(row < 3, 1.0, 0.0).astype(BF16)
    for h in range(N_HEADS):
        qtf_ref[h * LANES:h * LANES + HEAD_DIM, :] = qt[h * HEAD_DIM:(h + 1) * HEAD_DIM, :]
        qtf_ref[h * LANES + HEAD_DIM:(h + 1) * LANES, :] = ones_rows
    ktf32_ref[...] = nt(T_KF, T_VF)
    vt = nt(T_VF, T_KS)
    vtf32_ref[...] = vt
    vt16 = vt.astype(BF16)
    pad_row = lax.broadcasted_iota(jnp.int32, (VT_ROWS - HEAD_DIM, tm), 0)
    one_row = jnp.where(pad_row == 0, 1.0, 0.0).astype(BF16)
    for h in range(N_HEADS):
        vtf16_ref[h * VT_ROWS:h * VT_ROWS + HEAD_DIM, :] = vt16[h * HEAD_DIM:(h + 1) * HEAD_DIM, :]
        vtf16_ref[h * VT_ROWS + HEAD_DIM:(h + 1) * VT_ROWS, :] = one_row

    qs_ref[...] = (nn(P_QS, P_VS) * (SCALE * LOG2E)).astype(BF16)
    vs16_ref[...] = nn(P_VS, P_FL).astype(BF16)
    kt = nt(T_KS, T_VS)
    kts32_ref[...] = kt
    kts16_ref[...] = kt.astype(BF16)
    vts32_ref[...] = nt(T_VS, T_FL)

    gch = nn(P_CV, P_QS)
    u = gch[:, D_CONV:2 * D_CONV] * gch[:, 2 * D_CONV:]
    ubuf[8:tm + 8, :] = u
    y = (ubuf[6:tm + 6, :] * cw_ref[0:1, :] + ubuf[7:tm + 7, :] * cw_ref[1:2, :]
         + u * cw_ref[2:3, :])
    tail = u[tm - 8:tm, :]
    ubuf[0:8, :] = tail
    utail_ref[...] = tail
    oc_ref[...] = (_rms(gch[:, :D_CONV] * y) * gc_ref[...]).astype(BF16)


def _inproj_prompt(layer, depth, x, gn, wnn, wnt, bf, bft, cw, gc, tri, place, stacked, *, seq, tm):
    t = x.shape[0]
    n_seq = t // seq
    tps = seq // tm
    row = lambda width: pl.BlockSpec((tm, width), lambda i: (i, 0))
    colt = lambda rows: pl.BlockSpec((rows, tm), lambda i: (0, i))
    seqt = lambda rows: pl.BlockSpec((None, rows, tm), lambda i: (i // tps, 0, i % tps))
    stk = lambda rows: pl.BlockSpec((None, None, rows, tm), lambda i: (layer, i // tps, 0, i % tps))
    stacked_shapes = [jax.ShapeDtypeStruct((depth, n_seq, D_ATT, seq), F32)] * 4 + [
        jax.ShapeDtypeStruct((depth, 8, t), F32)]
    stacked_specs = [stk(D_ATT)] * 4 + [pl.BlockSpec((None, 8, tm), lambda i: (layer, 0, i))]
    out_shape = [jax.ShapeDtypeStruct((D_SLOT, t), BF16), jax.ShapeDtypeStruct((t, D_SLOT), BF16),
                 jax.ShapeDtypeStruct((n_seq, N_HEADS * VT_ROWS, seq), BF16),
                 jax.ShapeDtypeStruct((t, D_ATT), BF16),
                 jax.ShapeDtypeStruct((n_seq, D_ATT, seq), BF16), jax.ShapeDtypeStruct((t, D_ATT), BF16),
                 jax.ShapeDtypeStruct((t, D_CONV), BF16),
                 jax.ShapeDtypeStruct((n_seq * 8, D_CONV), F32)] + stacked_shapes
    out_specs = [colt(D_SLOT), row(D_SLOT), seqt(N_HEADS * VT_ROWS), row(D_ATT), seqt(D_ATT), row(D_ATT),
                 row(D_CONV), pl.BlockSpec((8, D_CONV), lambda i: (i // tps, 0))] + stacked_specs
    in_specs = [row(D_MODEL), _full((1, D_MODEL)), _full((D_MODEL, P_END)), _full((T_END, D_MODEL)),
                _full((1, LANES)), _full((HEAD_ROWS, 1)), _full((8, D_CONV)), _full((1, D_CONV)),
                _full((tm, tm)), _full((3, LANES, D_SLOT))]
    args = [x, gn, wnn, wnt, bf, bft, cw, gc, tri, place]
    aliases = {}
    n_alias = 0
    if stacked is not None:
        n_alias = len(stacked)
        in_specs += [pl.BlockSpec(memory_space=pl.ANY)] * n_alias
        aliases = {len(args) + k: 8 + k for k in range(n_alias)}
        args += list(stacked)
    return pl.pallas_call(
        functools.partial(_inproj_prompt_kernel, tm=tm, tiles_per_seq=tps, n_alias=n_alias),
        grid=(t // tm,), in_specs=in_specs, out_specs=out_specs, out_shape=out_shape,
        input_output_aliases=aliases,
        scratch_shapes=[pltpu.VMEM((tm + 8, D_CONV), F32), pltpu.VMEM((1, LANES), F32)],
        compiler_params=_params("arbitrary"), name="inproj_prompt",
    )(*args)


def _inproj_sample_kernel(x_ref, gn_ref, w_ref, bf_ref, cw_ref, gc_ref, p0_ref, p1_ref,
                          qf_ref, kf32_ref, vf32_ref, qs_ref, ks32_ref, vs32_ref,
                          logf_ref, oc_ref, u_ref):
    xb = (_rms(x_ref[...]) * gn_ref[...]).astype(BF16)

    def proj(a, b):
        return _dot(xb, w_ref[:, a:b])

    for c_q, q_ref, k_ref, v_ref in ((C_QF, qf_ref, kf32_ref, vf32_ref),
                                     (C_QS, qs_ref, ks32_ref, vs32_ref)):
        qkv = proj(c_q, c_q + 3 * D_ATT)
        q_ref[...] = qkv[:, :D_ATT] * SCALE
        k_ref[...] = qkv[:, D_ATT:2 * D_ATT]
        v_ref[...] = qkv[:, 2 * D_ATT:]
    fl = proj(C_FL, N_IN_PAD) + bf_ref[...]
    lane = lax.broadcasted_iota(jnp.int32, fl.shape, 1)
    logf_ref[...] = jnp.where(lane < N_HEADS, -_softplus(-fl), 0.0)
    gch = proj(C_GB, C_GB + 3 * D_CONV)
    u = gch[:, D_CONV:2 * D_CONV] * gch[:, 2 * D_CONV:]
    u_ref[...] = u
    y = p0_ref[...] * cw_ref[0:1, :] + p1_ref[...] * cw_ref[1:2, :] + u * cw_ref[2:3, :]
    oc_ref[...] = (_rms(gch[:, :D_CONV] * y) * gc_ref[...]).astype(BF16)


def _inproj_sample(x, gn, w, bf, cw, gc, p0, p1):
    t = x.shape[0]
    att = jax.ShapeDtypeStruct((t, D_ATT), F32)
    out_shape = [att] * 6 + [jax.ShapeDtypeStruct((t, LANES), F32),
                             jax.ShapeDtypeStruct((t, D_CONV), BF16),
                             jax.ShapeDtypeStruct((t, D_CONV), F32)]
    return pl.pallas_call(
        _inproj_sample_kernel, out_shape=out_shape,
        compiler_params=pltpu.CompilerParams(vmem_limit_bytes=VMEM_LIMIT), name="inproj_sample",
    )(x, gn, w, bf, cw, gc, p0, p1)


def _fox_prompt_kernel(qt_ref, k_ref, vt_ref, o_ref, sa_s, sb_s, m_s, acc_s, *, tq, tk):
    qi = pl.program_id(2)
    m_s[...] = jnp.full(m_s.shape, NEG, F32)
    acc_s[...] = jnp.zeros(acc_s.shape, F32)

    def scores(j, s_ref):
        start = pl.multiple_of(j * tk, tk)
        for h in range(2):
            k = k_ref[pl.ds(start, tk), h * LANES:(h + 1) * LANES]
            s_ref[h] = _dot(k, qt_ref[h * LANES:(h + 1) * LANES, :])

    def update(j, s_ref, masked):
        start = pl.multiple_of(j * tk, tk)
        if masked:
            key = lax.broadcasted_iota(jnp.int32, (tk, tq), 0) + start
            qry = lax.broadcasted_iota(jnp.int32, (tk, tq), 1) + qi * tq
            causal = key <= qry
        for h in range(2):
            s = s_ref[h]
            if masked:
                s = jnp.where(causal, s, NEG)
            m_old = m_s[h]
            m_new = jnp.maximum(m_old, jnp.max(s, axis=0, keepdims=True))
            alpha = jnp.exp2(m_old - m_new)
            p = jnp.exp2(s - m_new)
            vt = vt_ref[h * VT_ROWS:(h + 1) * VT_ROWS, pl.ds(start, tk)]
            acc_s[h] = alpha * acc_s[h] + _dot(vt, p.astype(BF16))
            m_s[h] = m_new

    assert tq == 2 * tk
    scores(0, sa_s)

    def pair(i):
        scores(2 * i + 1, sb_s)
        update(2 * i, sa_s, False)
        scores(2 * i + 2, sa_s)
        update(2 * i + 1, sb_s, False)

    def two_pairs(i, carry):
        pair(2 * i)
        pair(2 * i + 1)
        return carry

    lax.fori_loop(0, qi // 2, two_pairs, 0)

    @pl.when(qi % 2 == 1)
    def _():
        pair(qi - 1)

    scores(2 * qi + 1, sb_s)
    update(2 * qi, sa_s, True)
    update(2 * qi + 1, sb_s, True)

    o_t = jnp.concatenate([acc_s[h, :HEAD_DIM] / acc_s[h, HEAD_DIM:HEAD_DIM + 1] for h in range(2)],
                          axis=0)
    o_ref[...] = o_t.T


def _split_heads(q_ref, q_s):
    q = q_ref[...]
    lo = lax.broadcasted_iota(jnp.int32, q.shape, 1) < HEAD_DIM
    zero = jnp.zeros_like(q)
    q_s[0] = jnp.where(lo, q, zero)
    q_s[1] = jnp.where(lo, zero, q)


def _sb_prompt_kernel(q_ref, kt_ref, v_ref, su_ref, o_ref, q_s, za_s, zb_s, r_s, acc_s, *, tq, tk):
    qi = pl.program_id(2)
    _split_heads(q_ref, q_s)
    r_s[...] = jnp.zeros(r_s.shape, F32)
    acc_s[...] = jnp.zeros(acc_s.shape, F32)

    def logits(j, z_ref):
        start = pl.multiple_of(jnp.maximum(j, 0) * tk, tk)
        kt = kt_ref[:, pl.ds(start, tk)]
        for h in range(2):
            z_ref[h] = _dot(q_s[h], kt)

    def update(j, z_ref, masked):
        start = pl.multiple_of(j * tk, tk)
        v = v_ref[pl.ds(start, tk), :]
        su = su_ref[...]
        if masked:
            row = lax.broadcasted_iota(jnp.int32, (tq, tk), 0) + qi * tq
            col = lax.broadcasted_iota(jnp.int32, (tq, tk), 1) + start
            strict = col < row
        exponents = []
        for h in range(2):
            z = z_ref[h]
            neg_abs = pltpu.bitcast(pltpu.bitcast(z, jnp.int32) | SIGN_BIT, F32)
            ls = jnp.minimum(z, 0.0) - jnp.log2(1.0 + jnp.exp2(neg_abs))
            lk = ls - z
            if masked:
                lk = jnp.where(strict, lk, 0.0)
            r_old = r_s[h]
            after = _dot(lk.astype(BF16), su) + jnp.concatenate([r_old] * (tk // LANES), axis=1)
            exponents.append(ls + after)
            r_s[h] = r_old + jnp.sum(lk, axis=-1, keepdims=True)
        for h in range(2):
            a = jnp.exp2(exponents[h])
            if masked:
                a = jnp.where(strict, a, 0.0)
            acc_s[h] = acc_s[h] + _dot(a.astype(BF16), v)

    assert tq == 2 * tk
    top = 2 * qi + 1
    logits(top, za_s)
    logits(top - 1, zb_s)
    update(top, za_s, True)
    logits(top - 2, za_s)
    update(top - 1, zb_s, True)

    def pair(j):
        logits(j - 1, zb_s)
        update(j, za_s, False)
        logits(j - 2, za_s)
        update(j - 1, zb_s, False)

    def some_weight_nonzero():
        r_max = jnp.max(jnp.maximum(r_s[0], r_s[1]), axis=0, keepdims=True)
        return r_max[0, 0] > UNDERFLOW_LOG2

    def more(c):
        i, alive = c
        return jnp.logical_and(i < qi, alive)

    def step(c):
        i, _ = c
        pair(top - 2 - 2 * i)
        return i + 1, some_weight_nonzero()

    lax.while_loop(more, step, (jnp.int32(0), some_weight_nonzero()))

    lo = lax.broadcasted_iota(jnp.int32, (tq, LANES), 1) < HEAD_DIM
    o_ref[...] = jnp.where(lo, acc_s[0], acc_s[1])


def _fox_prompt(qt, kaug, vt, *, seq, tq, tk):
    t = kaug.shape[0]
    n_seq, nq = t // seq, seq // tq
    o_spec = pl.BlockSpec((tq, LANES), lambda b, p, i: (b * nq + i, p))
    return pl.pallas_call(
        functools.partial(_fox_prompt_kernel, tq=tq, tk=tk), grid=(n_seq, N_PAIRS, nq),
        in_specs=[pl.BlockSpec((2 * LANES, tq), lambda b, p, i: (p, b * nq + i)),
                  pl.BlockSpec((seq, 2 * LANES), lambda b, p, i: (b, p)),
                  pl.BlockSpec((None, 2 * VT_ROWS, seq), lambda b, p, i: (b, p, 0))],
        out_specs=o_spec, out_shape=jax.ShapeDtypeStruct((t, D_ATT), F32),
        scratch_shapes=[pltpu.VMEM((2, tk, tq), F32), pltpu.VMEM((2, tk, tq), F32),
                        pltpu.VMEM((2, 1, tq), F32), pltpu.VMEM((2, VT_ROWS, tq), F32)],
        compiler_params=_params("arbitrary", "arbitrary", "arbitrary"), name="fox_prompt",
    )(qt, kaug, vt)


def _sb_prompt(q, kt, v, su, *, seq, tq, tk):
    t = q.shape[0]
    n_seq, nq = t // seq, seq // tq
    q_spec = pl.BlockSpec((tq, LANES), lambda b, p, i: (b * nq + i, p))
    return pl.pallas_call(
        functools.partial(_sb_prompt_kernel, tq=tq, tk=tk), grid=(n_seq, N_PAIRS, nq),
        in_specs=[q_spec,
                  pl.BlockSpec((None, LANES, seq), lambda b, p, i: (b, p, 0)),
                  pl.BlockSpec((seq, LANES), lambda b, p, i: (b, p)),
                  pl.BlockSpec((tk, tk), lambda b, p, i: (0, 0))],
        out_specs=q_spec, out_shape=jax.ShapeDtypeStruct((t, D_ATT), F32),
        scratch_shapes=[pltpu.VMEM((2, tq, LANES), BF16), pltpu.VMEM((2, tq, tk), F32),
                        pltpu.VMEM((2, tq, tk), F32), pltpu.VMEM((2, tq, LANES), F32),
                        pltpu.VMEM((2, tq, LANES), F32)],
        compiler_params=_params("arbitrary", "arbitrary", "arbitrary"), name="sb_prompt",
    )(q, kt, v, su)


def _pick_rows(parts):
    rows = lax.broadcasted_iota(jnp.int32, parts[0].shape, 0)
    out = jnp.zeros_like(parts[0])
    for h, part in enumerate(parts):
        out = jnp.where(rows == h, part, out)
    return out


def _suffix_over_pages(x, su, carry):
    g_n = x.shape[1] // PAGE_SIZE
    stacked = jnp.concatenate([x[:, g * PAGE_SIZE:(g + 1) * PAGE_SIZE] for g in range(g_n)], axis=0)
    hi, mid, lo = _split3(stacked)
    inside = _dot(hi, su) + _dot(mid, su) + _dot(lo, su)
    total = jnp.sum(stacked, axis=-1, keepdims=True)
    outs = []
    for g in range(g_n):
        outs.append(inside[g * HEAD_ROWS:(g + 1) * HEAD_ROWS] + carry)
        carry = carry + total[g * HEAD_ROWS:(g + 1) * HEAD_ROWS]
    return jnp.concatenate(outs, axis=1), carry


def _wide(page_refs, h):
    return jnp.concatenate([r[h].astype(BF16) for r in page_refs], axis=1)


def _page_scores(q, page_refs):
    qb = q.astype(BF16)
    return _pick_rows([_dot(qb, _wide(page_refs, h)) for h in range(N_HEADS)])


def _decode_fox_kernel(pt_ref, qf_ref, knf_ref, vnf_ref, lfn_ref, su_ref, *refs):
    del pt_ref
    g_n = PAGES_PER_STEP
    kf, vf, lf = (refs[k * g_n:(k + 1) * g_n] for k in range(3))
    of_ref, m_s, l_s, g_s, acc_s = refs[3 * g_n:]
    p = pl.program_id(1)

    @pl.when(p == 0)
    def _():
        m_s[...] = jnp.full(m_s.shape, NEG, F32)
        l_s[...] = jnp.zeros(l_s.shape, F32)
        g_s[...] = jnp.zeros(g_s.shape, F32)
        acc_s[...] = jnp.zeros(acc_s.shape, F32)

    qf = qf_ref[...]
    zeros = jnp.zeros((HEAD_ROWS - 8, PAGE_SIZE), F32)
    lf_w = jnp.concatenate([jnp.concatenate([r[...], zeros], axis=0) for r in lf], axis=1)
    decay, g_new = _suffix_over_pages(lf_w, su_ref[...], g_s[...])
    g_s[...] = g_new
    s = _page_scores(qf, kf) + decay + lfn_ref[...]
    m_old = m_s[...]
    m_new = jnp.maximum(m_old, jnp.max(s, axis=-1, keepdims=True))
    alpha = jnp.exp(m_old - m_new)
    pr = jnp.exp(s - m_new)
    l_s[...] = alpha * l_s[...] + jnp.sum(pr, axis=-1, keepdims=True)
    m_s[...] = m_new
    prb = pr.astype(BF16)
    for h in range(N_HEADS):
        acc_s[h] = alpha * acc_s[h] + _dot_nt(prb, _wide(vf, h))

    @pl.when(p == pl.num_programs(1) - 1)
    def _():
        s_self = jnp.sum(qf * knf_ref[...], axis=-1, keepdims=True)
        m_old = m_s[...]
        m_new = jnp.maximum(m_old, s_self)
        alpha = jnp.exp(m_old - m_new)
        p_self = jnp.exp(s_self - m_new)
        inv_l = 1.0 / (alpha * l_s[...] + p_self)
        own = p_self * vnf_ref[...]
        of_ref[...] = _pick_rows([alpha * acc_s[h] + own for h in range(N_HEADS)]) * inv_l


def _decode_sb_kernel(pt_ref, qs_ref, su_ref, *refs):
    del pt_ref
    g_n = PAGES_PER_STEP
    ks, vs = refs[:g_n], refs[g_n:2 * g_n]
    os_ref, carry_ref, r_s, acc_s = refs[2 * g_n:]
    p = pl.program_id(1)

    @pl.when(p == 0)
    def _():
        r_s[...] = jnp.zeros(r_s.shape, F32)
        acc_s[...] = jnp.zeros(acc_s.shape, F32)

    @pl.when(jnp.max(r_s[...], axis=0, keepdims=True)[0, 0] > UNDERFLOW_LN)
    def _():
        z = _page_scores(qs_ref[...], ks)
        lk = -_softplus(z)
        after, r_new = _suffix_over_pages(lk, su_ref[...], r_s[...])
        r_s[...] = r_new
        ab = jnp.exp(z + lk + after).astype(BF16)
        for h in range(N_HEADS):
            acc_s[h] = acc_s[h] + _dot_nt(ab, _wide(vs, h))

    @pl.when(p == pl.num_programs(1) - 1)
    def _():
        os_ref[...] = _pick_rows([acc_s[h] for h in range(N_HEADS)])
        carry_ref[...] = r_s[...]


def _decode_specs(layer, page_table):
    n_b, n_pages = page_table.shape
    g_n = PAGES_PER_STEP
    per_b = lambda shape: pl.BlockSpec((None,) + shape, lambda b, p, pt: (b,) + (0,) * len(shape))

    def pages(shape):
        return [pl.BlockSpec((None, None) + shape,
                             lambda b, p, pt, g=g: (layer, pt[b, n_pages - 1 - (p * g_n + g)])
                             + (0,) * len(shape)) for g in range(g_n)]

    su_spec = pl.BlockSpec((PAGE_SIZE, PAGE_SIZE), lambda b, p, pt: (0, 0))
    return per_b, pages, su_spec


def _decode_fox(layer, page_table, qf, knf, vnf, lfn, su, cache_kf, cache_vf, cache_lf):
    n_b, n_pages = page_table.shape
    g_n = PAGES_PER_STEP
    per_b, pages, su_spec = _decode_specs(layer, page_table)
    head_blk = per_b((HEAD_ROWS, HEAD_DIM))
    kv_pages = pages((N_HEADS, HEAD_DIM, PAGE_SIZE))
    col = pltpu.VMEM((HEAD_ROWS, 1), F32)
    return pl.pallas_call(
        _decode_fox_kernel,
        grid_spec=pltpu.PrefetchScalarGridSpec(
            num_scalar_prefetch=1, grid=(n_b, n_pages // g_n),
            in_specs=[head_blk, head_blk, head_blk, per_b((HEAD_ROWS, 1)), su_spec]
            + kv_pages + kv_pages + pages((8, PAGE_SIZE)),
            out_specs=head_blk,
            scratch_shapes=[col, col, col, pltpu.VMEM((N_HEADS, HEAD_ROWS, HEAD_DIM), F32)]),
        out_shape=jax.ShapeDtypeStruct((n_b, HEAD_ROWS, HEAD_DIM), F32),
        compiler_params=_params("arbitrary", "arbitrary"), name="decode_fox",
    )(page_table, qf, knf, vnf, lfn, su, *([cache_kf] * g_n), *([cache_vf] * g_n),
      *([cache_lf] * g_n))


def _decode_sb(layer, page_table, qs, su, cache_ks, cache_vs, *, n_steps):
    n_b = page_table.shape[0]
    g_n = PAGES_PER_STEP
    per_b, pages, su_spec = _decode_specs(layer, page_table)
    head_blk = per_b((HEAD_ROWS, HEAD_DIM))
    col_blk = per_b((HEAD_ROWS, 1))
    kv_pages = pages((N_HEADS, HEAD_DIM, PAGE_SIZE))
    return pl.pallas_call(
        _decode_sb_kernel,
        grid_spec=pltpu.PrefetchScalarGridSpec(
            num_scalar_prefetch=1, grid=(n_b, n_steps),
            in_specs=[head_blk, su_spec] + kv_pages + kv_pages,
            out_specs=[head_blk, col_blk],
            scratch_shapes=[pltpu.VMEM((HEAD_ROWS, 1), F32),
                            pltpu.VMEM((N_HEADS, HEAD_ROWS, HEAD_DIM), F32)]),
        out_shape=[jax.ShapeDtypeStruct((n_b, HEAD_ROWS, HEAD_DIM), F32),
                   jax.ShapeDtypeStruct((n_b, HEAD_ROWS, 1), F32)],
        compiler_params=_params("arbitrary", "arbitrary"), name="decode_sb",
    )(page_table, qs, su, *([cache_ks] * g_n), *([cache_vs] * g_n))


def _decode_sb_all(layer, page_table, qs, su, cache_ks, cache_vs):
    n_steps = page_table.shape[1] // PAGES_PER_STEP
    o_new, carry = _decode_sb(layer, page_table, qs, su, cache_ks, cache_vs, n_steps=1)
    if n_steps == 1:
        return o_new
    return lax.cond(
        jnp.max(carry) > UNDERFLOW_LN,
        lambda: _decode_sb(layer, page_table, qs, su, cache_ks, cache_vs, n_steps=n_steps)[0],
        lambda: o_new)


def _outproj_kernel(of_ref, oc_ref, os_ref, x_ref, gf_ref, gs_ref, w_ref, o_ref):
    nf = (_rms(of_ref[...]) * gf_ref[...]).astype(BF16)
    ns = (_rms(os_ref[...]) * gs_ref[...]).astype(BF16)
    y = (_dot(nf, w_ref[0:D_ATT, :]) + _dot(oc_ref[...], w_ref[D_ATT:D_ATT + D_CONV, :])
         + _dot(ns, w_ref[D_ATT + D_CONV:, :]))
    o_ref[...] = x_ref[...] + y


def _outproj(of, oc, os_, x, gf, gs, w, *, tm):
    t = x.shape[0]
    row = lambda width: pl.BlockSpec((tm, width), lambda i: (i, 0))
    return pl.pallas_call(
        _outproj_kernel, grid=(t // tm,),
        in_specs=[row(D_ATT), row(D_CONV), row(D_ATT), row(D_MODEL), _full((1, D_ATT)),
                  _full((1, D_ATT)), _full((D_MODEL, D_MODEL))],
        out_specs=row(D_MODEL), out_shape=jax.ShapeDtypeStruct((t, D_MODEL), F32),
        compiler_params=_params("arbitrary"), name="outproj",
    )(of, oc, os_, x, gf, gs, w)


def _route(logits):
    lane_i = lax.broadcasted_iota(jnp.int32, logits.shape, 1)
    lane = lane_i.astype(F32)
    lane_group = jnp.right_shift(lane_i, EXPERTS_PER_GROUP.bit_length() - 1).astype(F32)
    big = float(LANES)
    is_group = (lane_i >= N_EXPERTS) & (lane_i < N_EXPERTS + N_GROUPS)
    gl = jnp.where(is_group, logits, NEG)
    g_max = jnp.max(gl, axis=-1, keepdims=True)
    g_idx = jnp.min(jnp.where(gl == g_max, lane, big), axis=-1, keepdims=True) - N_EXPERTS
    p_group = 1.0 / jnp.sum(jnp.exp(gl - g_max), axis=-1, keepdims=True)
    in_group = (lane_i < N_EXPERTS) & (lane_group == g_idx)
    el = jnp.where(in_group, logits, NEG)
    m1 = jnp.max(el, axis=-1, keepdims=True)
    i1 = jnp.min(jnp.where(el == m1, lane, big), axis=-1, keepdims=True)
    el2 = jnp.where(lane == i1, NEG, el)
    m2 = jnp.max(el2, axis=-1, keepdims=True)
    i2 = jnp.min(jnp.where(el2 == m2, lane, big), axis=-1, keepdims=True)
    e21 = jnp.exp(m2 - m1)
    w1 = p_group / (1.0 + e21)
    return jnp.where(lane == i1, w1, 0.0) + jnp.where(lane == i2, w1 * e21, 0.0)


def _moe_kernel(x_ref, gn_ref, wr_ref, br_ref, wg_ref, wu_ref, wd_ref, gfin_ref, o_ref,
                xn_s, gate_s, *, final):
    g = pl.program_id(1)

    @pl.when(g == 0)
    def _():
        x = x_ref[...]
        xn = _rms(x) * gn_ref[...]
        xn_s[...] = xn.astype(BF16)
        logits = jnp.dot(xn, wr_ref[...], preferred_element_type=F32,
                         precision=lax.Precision.HIGHEST) + br_ref[...]
        gate = _route(logits)
        for gi in range(N_GROUPS):
            shift = (LANES - gi * EXPERTS_PER_GROUP) % LANES
            gate_s[gi] = pltpu.roll(gate, shift, 1) if shift else gate
        o_ref[...] = x

    xb = xn_s[...]
    hg = _dot(xb, wg_ref[...])
    hu = _dot(xb, wu_ref[...])
    h = hg * (1.0 / (1.0 + jnp.exp(-hg))) * hu
    gate = gate_s[g]
    hgated = jnp.concatenate(
        [h[:, e * D_EXPERT:(e + 1) * D_EXPERT] * gate[:, e:e + 1] for e in range(EXPERTS_PER_GROUP)],
        axis=1).astype(BF16)
    o_ref[...] += _dot(hgated, wd_ref[...])

    if final:
        @pl.when(g == N_GROUPS - 1)
        def _():
            o_ref[...] = _rms(o_ref[...]) * gfin_ref[...]


def _moe(x, gn, wr, br, wg, wu, wd, gfin, *, tm, final):
    t = x.shape[0]
    row = pl.BlockSpec((tm, D_MODEL), lambda i, g: (i, 0))
    const = lambda shape: pl.BlockSpec(shape, lambda i, g: (0,) * len(shape))
    return pl.pallas_call(
        functools.partial(_moe_kernel, final=final), grid=(t // tm, N_GROUPS),
        in_specs=[row, const((1, D_MODEL)), const((D_MODEL, LANES)), const((1, LANES)),
                  pl.BlockSpec((D_MODEL, D_GROUP), lambda i, g: (0, g)),
                  pl.BlockSpec((D_MODEL, D_GROUP), lambda i, g: (0, g)),
                  pl.BlockSpec((D_GROUP, D_MODEL), lambda i, g: (g, 0)),
                  const((1, D_MODEL))],
        out_specs=row, out_shape=jax.ShapeDtypeStruct((t, D_MODEL), F32),
        scratch_shapes=[pltpu.VMEM((tm, D_MODEL), BF16), pltpu.VMEM((N_GROUPS, tm, LANES), F32)],
        compiler_params=_params("arbitrary", "arbitrary"), name="moe",
    )(x, gn, wr, br, wg, wu, wd, gfin)


def _tri(n, rel):
    r = lax.broadcasted_iota(jnp.int32, (n, n), 0)
    c = lax.broadcasted_iota(jnp.int32, (n, n), 1)
    return rel(r, c).astype(BF16)


def kernel(x_prompt, x_sample, cache_k_fox, cache_v_fox, cache_logf_fox, cache_k_sb, cache_v_sb,
           state_conv, page_table, g_mix_norm, w_in, b_f, conv_w, g_mix_out, w_out, g_ffn_norm,
           w_router_g, b_router_g, w_router_e, b_router_e, w_gate, w_up, w_down, g_final):
    depth = w_in.shape[0]
    n_seq, seq, _ = x_prompt.shape
    n_dec = x_sample.shape[0]
    t_p = n_seq * seq
    tm_in, tm_out, tm_moe = 256, 512, 512
    tq_f, tk_f, tq_s, tk_s = 512, 256, 512, 256

    c_fl = 3 * D_ATT
    c_end = c_fl + N_HEADS
    wq_f, wk_f, wv_f = w_in[..., :D_ATT], w_in[..., D_ATT:2 * D_ATT], w_in[..., 2 * D_ATT:c_fl]
    w_f = w_in[..., c_fl:c_end]
    w_cv = w_in[..., c_end:c_end + 3 * D_CONV]
    c_s = c_end + 3 * D_CONV
    wq_s, wk_s, wv_s = (w_in[..., c_s:c_s + D_ATT], w_in[..., c_s + D_ATT:c_s + 2 * D_ATT],
                        w_in[..., c_s + 2 * D_ATT:])
    pad_cols = lambda w, n: jnp.pad(w, ((0, 0), (0, 0), (0, n - w.shape[-1])))
    w_in_r = jnp.concatenate([wq_f, wk_f, wv_f, w_cv, wq_s, wk_s, wv_s, pad_cols(w_f, LANES)],
                             axis=-1).astype(BF16)
    wk_slots = jnp.pad(wk_f.reshape(depth, D_MODEL, N_HEADS, HEAD_DIM),
                       ((0, 0), (0, 0), (0, 0), (0, LANES - HEAD_DIM))).reshape(depth, D_MODEL, D_SLOT)
    w_nn = jnp.concatenate([wk_slots, w_cv, wq_s, wv_s, pad_cols(w_f, LANES)], axis=-1).astype(BF16)
    w_nt = jnp.swapaxes(jnp.concatenate([wq_f, wk_f, wv_f, wk_s, wv_s, pad_cols(w_f, HEAD_ROWS)],
                                        axis=-1), 1, 2).astype(BF16)
    bf_r = jnp.pad(b_f, ((0, 0), (0, LANES - N_HEADS)))[:, None, :]
    bft_r = jnp.pad(b_f, ((0, 0), (0, HEAD_ROWS - N_HEADS)))[:, :, None]
    cw_r = jnp.pad(conv_w, ((0, 0), (0, 8 - CONV_WIDTH), (0, 0)))
    g_f = g_mix_out[:, None, :D_ATT]
    g_c = g_mix_out[:, None, D_ATT:D_ATT + D_CONV]
    g_s = g_mix_out[:, None, D_ATT + D_CONV:]
    w_out_r = w_out.astype(BF16)
    w_r = jnp.concatenate(
        [w_router_e, w_router_g,
         jnp.zeros((depth, D_MODEL, LANES - N_EXPERTS - N_GROUPS), F32)], axis=-1)
    b_r = jnp.concatenate(
        [b_router_e, b_router_g, jnp.zeros((depth, LANES - N_EXPERTS - N_GROUPS), F32)],
        axis=-1)[:, None, :]
    w_gate_r = jnp.transpose(w_gate, (0, 2, 1, 3)).reshape(depth, D_MODEL, -1).astype(BF16)
    w_up_r = jnp.transpose(w_up, (0, 2, 1, 3)).reshape(depth, D_MODEL, -1).astype(BF16)
    w_down_r = w_down.reshape(depth, -1, D_MODEL).astype(BF16)
    g_mix = g_mix_norm[:, None, :]
    g_ffn = g_ffn_norm[:, None, :]
    g_fin = g_final[None, :]

    tri_in = _tri(tm_in, lambda r, c: c <= r)
    su_sb = _tri(tk_s, lambda r, c: r > c)
    su_page = _tri(PAGE_SIZE, lambda r, c: r > c)
    pr = lax.broadcasted_iota(jnp.int32, (3, LANES, D_SLOT), 1)
    pc = lax.broadcasted_iota(jnp.int32, (3, LANES, D_SLOT), 2)
    pi = lax.broadcasted_iota(jnp.int32, (3, LANES, D_SLOT), 0)
    place = ((pr < N_HEADS) & (pc == pr * LANES + HEAD_DIM + pi)).astype(BF16)

    to_lanes = lambda c: jnp.transpose(c, (0, 1, 3, 4, 2))
    ck_f, cv_f, ck_s, cv_s = map(to_lanes, (cache_k_fox, cache_v_fox, cache_k_sb, cache_v_sb))
    c_lf = jnp.pad(jnp.swapaxes(cache_logf_fox, 2, 3), ((0, 0), (0, 0), (0, 8 - N_HEADS), (0, 0)))

    def heads16(a):
        return jnp.pad(a.reshape(n_dec, N_HEADS, HEAD_DIM), ((0, 0), (0, HEAD_ROWS - N_HEADS), (0, 0)))

    xp = x_prompt.reshape(t_p, D_MODEL)
    xs = x_sample.reshape(n_dec, D_MODEL)
    stacked = None
    conv_p = []
    st_s = [[] for _ in range(6)]
    for l in range(depth):
        final = l == depth - 1
        outs = _inproj_prompt(l, depth, xp, g_mix[l], w_nn[l], w_nt[l], bf_r[l], bft_r[l], cw_r[l],
                              g_c[l], tri_in, place, stacked, seq=seq, tm=tm_in)
        qtf, kaug, vtf16, qs, kts16, vs16, oc, utail = outs[:8]
        stacked = outs[8:]
        of = _fox_prompt(qtf, kaug, vtf16, seq=seq, tq=tq_f, tk=tk_f)
        os_ = _sb_prompt(qs, kts16, vs16, su_sb, seq=seq, tq=tq_s, tk=tk_s)
        xp = _outproj(of, oc, os_, xp, g_f[l], g_s[l], w_out_r[l], tm=tm_out)
        xp = _moe(xp, g_ffn[l], w_r[l], b_r[l], w_gate_r[l], w_up_r[l], w_down_r[l], g_fin,
                  tm=tm_moe, final=final)
        conv_p.append(utail.reshape(n_seq, 8, D_CONV)[:, 8 - (CONV_WIDTH - 1):])
        prev = state_conv[l]
        (qf, kf32, vf32, qs, ks32, vs32, logf, oc, u) = _inproj_sample(
            xs, g_mix[l], w_in_r[l], bf_r[l], cw_r[l], g_c[l], prev[:, 0], prev[:, 1])
        lfn = jnp.pad(logf[:, :N_HEADS], ((0, 0), (0, HEAD_ROWS - N_HEADS)))[:, :, None]
        of = _decode_fox(l, page_table, heads16(qf), heads16(kf32), heads16(vf32), lfn, su_page,
                         ck_f, cv_f, c_lf)
        os_ = _decode_sb_all(l, page_table, heads16(qs), su_page, ck_s, cv_s)
        of = of[:, :N_HEADS].reshape(n_dec, D_ATT)
        os_ = os_[:, :N_HEADS].reshape(n_dec, D_ATT)
        xs = _outproj(of, oc, os_, xs, g_f[l], g_s[l], w_out_r[l], tm=n_dec)
        xs = _moe(xs, g_ffn[l], w_r[l], b_r[l], w_gate_r[l], w_up_r[l], w_down_r[l], g_fin,
                  tm=n_dec, final=final)
        hd1 = lambda a: a.reshape(n_dec, 1, N_HEADS, HEAD_DIM)
        new_s = (hd1(kf32), hd1(vf32), logf[:, :N_HEADS].reshape(n_dec, 1, N_HEADS), hd1(ks32), hd1(vs32),
                 jnp.stack([prev[:, 1], u], axis=1))
        for acc, a in zip(st_s, new_s):
            acc.append(a)

    kt_f, vt_f, kt_s, vt_s, lf_t = stacked
    from_lanes = lambda a: jnp.transpose(a.reshape(depth, n_seq, N_HEADS, HEAD_DIM, seq), (0, 1, 4, 2, 3))
    logf_p = jnp.transpose(lf_t[:, :N_HEADS].reshape(depth, N_HEADS, n_seq, seq), (0, 2, 3, 1))
    y_prompt = xp.reshape(n_seq, seq, D_MODEL)
    y_sample = xs.reshape(n_dec, 1, D_MODEL)
    return ((y_prompt, y_sample, from_lanes(kt_f), from_lanes(vt_f), logf_p, from_lanes(kt_s),
             from_lanes(vt_s), jnp.stack(conv_p)) + tuple(jnp.stack(a) for a in st_s))
```

```python
import functools
import math

import jax
import jax.numpy as jnp
from jax import lax
from jax.experimental import pallas as pl
from jax.experimental.pallas import tpu as pltpu

D_MODEL = 1024
HEAD_DIM = 64
N_HEADS = 6
N_PAIRS = N_HEADS // 2
D_ATT = N_HEADS * HEAD_DIM
D_CONV = 256
CONV_WIDTH = 3
N_GROUPS = 4
EXPERTS_PER_GROUP = 8
N_EXPERTS = N_GROUPS * EXPERTS_PER_GROUP
D_EXPERT = 128
D_GROUP = EXPERTS_PER_GROUP * D_EXPERT
PAGE_SIZE = 128
EPS = 1e-6
SCALE = HEAD_DIM ** -0.5
LOG2E = math.log2(math.e)
NEG = -1e30
SIGN_BIT = -2 ** 31
UNDERFLOW_LOG2 = -160.0
UNDERFLOW_LN = UNDERFLOW_LOG2 / LOG2E

LANES = 128
HEAD_ROWS = 16
D_SLOT = N_HEADS * LANES
VT_ROWS = HEAD_DIM + HEAD_ROWS
PAGES_PER_STEP = 16
SB_FIRST_PAGES = 4
VMEM_LIMIT = 56 * 1024 * 1024

C_QF, C_KF, C_VF = 0, 384, 768
C_GB, C_GC, C_HC = 1152, 1408, 1664
C_QS, C_KS, C_VS = 1920, 2304, 2688
C_FL = 3072
N_IN_PAD = 3200
P_KA, P_CV, P_QS, P_VS, P_FL, P_END = 0, 768, 1536, 1920, 2304, 2432
T_QF, T_KF, T_VF, T_KS, T_VS, T_FL, T_END = 0, 384, 768, 1152, 1536, 1920, 1936

F32 = jnp.float32
BF16 = jnp.bfloat16


def _rms(x):
    return x * lax.rsqrt(jnp.sum(x * x, axis=-1, keepdims=True) * (1.0 / x.shape[-1]) + EPS)


def _softplus(z):
    return jnp.maximum(z, 0.0) + jnp.log(1.0 + jnp.exp(-jnp.abs(z)))


def _split3(x):
    hi = x.astype(BF16)
    r = x - hi.astype(F32)
    mid = r.astype(BF16)
    lo = (r - mid.astype(F32)).astype(BF16)
    return hi, mid, lo


def _split2(x):
    hi = x.astype(BF16)
    return hi, (x - hi.astype(F32)).astype(BF16)


def _dot(a, b):
    return jnp.dot(a, b, preferred_element_type=F32)


def _dot_nt(a, b):
    return lax.dot_general(a, b, (((1,), (1,)), ((), ())), preferred_element_type=F32)


def _params(*sem):
    return pltpu.CompilerParams(dimension_semantics=sem, vmem_limit_bytes=VMEM_LIMIT)


def _full(shape):
    return pl.BlockSpec(shape, lambda *_: (0,) * len(shape))


def _inproj_prompt_kernel(*refs, tm, tiles_per_seq, n_alias):
    (x_ref, gn_ref, wnn_ref, wnt_ref, bf_ref, bft_ref, cw_ref, gc_ref, tri_ref,
     place_ref) = refs[:10]
    (qtf_ref, kaug_ref, vtf16_ref, qs_ref, kts16_ref, vs16_ref, oc_ref, utail_ref,
     ktf32_ref, vtf32_ref, kts32_ref, vts32_ref, lft_ref, ubuf, fcarry) = refs[10 + n_alias:]
    i = pl.program_id(0)
    xb = (_rms(x_ref[...]) * gn_ref[...]).astype(BF16)

    def nn(a, b):
        return _dot(xb, wnn_ref[:, a:b])

    def nt(a, b):
        return _dot_nt(wnt_ref[a:b, :], xb)

    @pl.when(i % tiles_per_seq == 0)
    def _():
        ubuf[0:8, :] = jnp.zeros((8, D_CONV), F32)
        fcarry[...] = jnp.zeros((1, LANES), F32)

    fl = nn(P_FL, P_END) + bf_ref[...]
    lane = lax.broadcasted_iota(jnp.int32, fl.shape, 1)
    logf = jnp.where(lane < N_HEADS, -_softplus(-fl), 0.0)
    tri = tri_ref[...]
    hi, mid, lo = _split3(logf)
    fcum = _dot(tri, hi) + _dot(tri, mid) + _dot(tri, lo) + fcarry[...]
    fcarry[...] = fcum[tm - 1:tm, :]
    lft_ref[...] = -_softplus(-(nt(T_FL, T_END) + bft_ref[...]))[0:8, :]

    hi, mid, lo = _split3(fcum * (-LOG2E))
    kaug = (nn(P_KA, P_CV) + _dot(hi, place_ref[0]) + _dot(mid, place_ref[1])
            + _dot(lo, place_ref[2]))
    kaug_ref[...] = kaug.astype(BF16)
    qt = (nt(T_QF, T_KF) * (SCALE * LOG2E)).astype(BF16)
    row = lax.broadcasted_iota(jnp.int32, (HEAD_DIM, tm), 0)
    ones_rows = jnp.where(row < 3, 1.0, 0.0).astype(BF16)
    for h in range(N_HEADS):
        qtf_ref[h * LANES:h * LANES + HEAD_DIM, :] = qt[h * HEAD_DIM:(h + 1) * HEAD_DIM, :]
        qtf_ref[h * LANES + HEAD_DIM:(h + 1) * LANES, :] = ones_rows
    ktf32_ref[...] = nt(T_KF, T_VF)
    vt = nt(T_VF, T_KS)
    vtf32_ref[...] = vt
    vt16 = vt.astype(BF16)
    pad_row = lax.broadcasted_iota(jnp.int32, (VT_ROWS - HEAD_DIM, tm), 0)
    one_row = jnp.where(pad_row == 0, 1.0, 0.0).astype(BF16)
    for h in range(N_HEADS):
        vtf16_ref[h * VT_ROWS:h * VT_ROWS + HEAD_DIM, :] = vt16[h * HEAD_DIM:(h + 1) * HEAD_DIM, :]
        vtf16_ref[h * VT_ROWS + HEAD_DIM:(h + 1) * VT_ROWS, :] = one_row

    qs_ref[...] = (nn(P_QS, P_VS) * (SCALE * LOG2E)).astype(BF16)
    vs16_ref[...] = nn(P_VS, P_FL).astype(BF16)
    kt = nt(T_KS, T_VS)
    kts32_ref[...] = kt
    kts16_ref[...] = kt.astype(BF16)
    vts32_ref[...] = nt(T_VS, T_FL)

    gch = nn(P_CV, P_QS)
    u = gch[:, D_CONV:2 * D_CONV] * gch[:, 2 * D_CONV:]
    ubuf[8:tm + 8, :] = u
    y = (ubuf[6:tm + 6, :] * cw_ref[0:1, :] + ubuf[7:tm + 7, :] * cw_ref[1:2, :]
         + u * cw_ref[2:3, :])
    tail = u[tm - 8:tm, :]
    ubuf[0:8, :] = tail
    utail_ref[...] = tail
    oc_ref[...] = (_rms(gch[:, :D_CONV] * y) * gc_ref[...]).astype(BF16)


def _inproj_prompt(layer, depth, x, gn, wnn, wnt, bf, bft, cw, gc, tri, place, stacked, *, seq, tm):
    t = x.shape[0]
    n_seq = t // seq
    tps = seq // tm
    row = lambda width: pl.BlockSpec((tm, width), lambda i: (i, 0))
    colt = lambda rows: pl.BlockSpec((rows, tm), lambda i: (0, i))
    seqt = lambda rows: pl.BlockSpec((None, rows, tm), lambda i: (i // tps, 0, i % tps))
    stk = lambda rows: pl.BlockSpec((None, None, rows, tm), lambda i: (layer, i // tps, 0, i % tps))
    stacked_shapes = [jax.ShapeDtypeStruct((depth, n_seq, D_ATT, seq), F32)] * 4 + [
        jax.ShapeDtypeStruct((depth, 8, t), F32)]
    stacked_specs = [stk(D_ATT)] * 4 + [pl.BlockSpec((None, 8, tm), lambda i: (layer, 0, i))]
    out_shape = [jax.ShapeDtypeStruct((D_SLOT, t), BF16), jax.ShapeDtypeStruct((t, D_SLOT), BF16),
                 jax.ShapeDtypeStruct((n_seq, N_HEADS * VT_ROWS, seq), BF16),
                 jax.ShapeDtypeStruct((t, D_ATT), BF16),
                 jax.ShapeDtypeStruct((n_seq, D_ATT, seq), BF16), jax.ShapeDtypeStruct((t, D_ATT), BF16),
                 jax.ShapeDtypeStruct((t, D_CONV), BF16),
                 jax.ShapeDtypeStruct((n_seq * 8, D_CONV), F32)] + stacked_shapes
    out_specs = [colt(D_SLOT), row(D_SLOT), seqt(N_HEADS * VT_ROWS), row(D_ATT), seqt(D_ATT), row(D_ATT),
                 row(D_CONV), pl.BlockSpec((8, D_CONV), lambda i: (i // tps, 0))] + stacked_specs
    in_specs = [row(D_MODEL), _full((1, D_MODEL)), _full((D_MODEL, P_END)), _full((T_END, D_MODEL)),
                _full((1, LANES)), _full((HEAD_ROWS, 1)), _full((8, D_CONV)), _full((1, D_CONV)),
                _full((tm, tm)), _full((3, LANES, D_SLOT))]
    args = [x, gn, wnn, wnt, bf, bft, cw, gc, tri, place]
    aliases = {}
    n_alias = 0
    if stacked is not None:
        n_alias = len(stacked)
        in_specs += [pl.BlockSpec(memory_space=pl.ANY)] * n_alias
        aliases = {len(args) + k: 8 + k for k in range(n_alias)}
        args += list(stacked)
    return pl.pallas_call(
        functools.partial(_inproj_prompt_kernel, tm=tm, tiles_per_seq=tps, n_alias=n_alias),
        grid=(t // tm,), in_specs=in_specs, out_specs=out_specs, out_shape=out_shape,
        input_output_aliases=aliases,
        scratch_shapes=[pltpu.VMEM((tm + 8, D_CONV), F32), pltpu.VMEM((1, LANES), F32)],
        compiler_params=_params("arbitrary"), name="inproj_prompt",
    )(*args)


def _inproj_sample_kernel(x_ref, gn_ref, w_ref, bf_ref, cw_ref, gc_ref, p0_ref, p1_ref,
                          qf_ref, kf32_ref, vf32_ref, qs_ref, ks32_ref, vs32_ref,
                          logf_ref, oc_ref, u_ref):
    xb = (_rms(x_ref[...]) * gn_ref[...]).astype(BF16)

    def proj(a, b):
        return _dot(xb, w_ref[:, a:b])

    for c_q, q_ref, k_ref, v_ref in ((C_QF, qf_ref, kf32_ref, vf32_ref),
                                     (C_QS, qs_ref, ks32_ref, vs32_ref)):
        qkv = proj(c_q, c_q + 3 * D_ATT)
        q_ref[...] = qkv[:, :D_ATT] * SCALE
        k_ref[...] = qkv[:, D_ATT:2 * D_ATT]
        v_ref[...] = qkv[:, 2 * D_ATT:]
    fl = proj(C_FL, N_IN_PAD) + bf_ref[...]
    lane = lax.broadcasted_iota(jnp.int32, fl.shape, 1)
    logf_ref[...] = jnp.where(lane < N_HEADS, -_softplus(-fl), 0.0)
    gch = proj(C_GB, C_GB + 3 * D_CONV)
    u = gch[:, D_CONV:2 * D_CONV] * gch[:, 2 * D_CONV:]
    u_ref[...] = u
    y = p0_ref[...] * cw_ref[0:1, :] + p1_ref[...] * cw_ref[1:2, :] + u * cw_ref[2:3, :]
    oc_ref[...] = (_rms(gch[:, :D_CONV] * y) * gc_ref[...]).astype(BF16)


def _inproj_sample(x, gn, w, bf, cw, gc, p0, p1):
    t = x.shape[0]
    att = jax.ShapeDtypeStruct((t, D_ATT), F32)
    out_shape = [att] * 6 + [jax.ShapeDtypeStruct((t, LANES), F32),
                             jax.ShapeDtypeStruct((t, D_CONV), BF16),
                             jax.ShapeDtypeStruct((t, D_CONV), F32)]
    return pl.pallas_call(
        _inproj_sample_kernel, out_shape=out_shape,
        compiler_params=pltpu.CompilerParams(vmem_limit_bytes=VMEM_LIMIT), name="inproj_sample",
    )(x, gn, w, bf, cw, gc, p0, p1)


def _fox_prompt_kernel(qt_ref, k_ref, vt_ref, o_ref, sa_s, sb_s, m_s, acc_s, *, tq, tk):
    qi = pl.program_id(2)
    m_s[...] = jnp.full(m_s.shape, NEG, F32)
    acc_s[...] = jnp.zeros(acc_s.shape, F32)

    def scores(j, s_ref):
        start = pl.multiple_of(j * tk, tk)
        for h in range(2):
            k = k_ref[pl.ds(start, tk), h * LANES:(h + 1) * LANES]
            s_ref[h] = _dot(k, qt_ref[h * LANES:(h + 1) * LANES, :])

    def update(j, s_ref, masked):
        start = pl.multiple_of(j * tk, tk)
        if masked:
            key = lax.broadcasted_iota(jnp.int32, (tk, tq), 0) + start
            qry = lax.broadcasted_iota(jnp.int32, (tk, tq), 1) + qi * tq
            causal = key <= qry
        for h in range(2):
            s = s_ref[h]
            if masked:
                s = jnp.where(causal, s, NEG)
            m_old = m_s[h]
            m_new = jnp.maximum(m_old, jnp.max(s, axis=0, keepdims=True))
            alpha = jnp.exp2(m_old - m_new)
            p = jnp.exp2(s - m_new)
            vt = vt_ref[h * VT_ROWS:(h + 1) * VT_ROWS, pl.ds(start, tk)]
            acc_s[h] = alpha * acc_s[h] + _dot(vt, p.astype(BF16))
            m_s[h] = m_new

    assert tq == 2 * tk
    scores(0, sa_s)

    def pair(i):
        scores(2 * i + 1, sb_s)
        update(2 * i, sa_s, False)
        scores(2 * i + 2, sa_s)
        update(2 * i + 1, sb_s, False)

    def two_pairs(i, carry):
        pair(2 * i)
        pair(2 * i + 1)
        return carry

    lax.fori_loop(0, qi // 2, two_pairs, 0)

    @pl.when(qi % 2 == 1)
    def _():
        pair(qi - 1)

    scores(2 * qi + 1, sb_s)
    update(2 * qi, sa_s, True)
    update(2 * qi + 1, sb_s, True)

    o_t = jnp.concatenate([acc_s[h, :HEAD_DIM] / acc_s[h, HEAD_DIM:HEAD_DIM + 1] for h in range(2)],
                          axis=0)
    o_ref[...] = o_t.T


def _split_heads(q_ref, q_s):
    q = q_ref[...]
    lo = lax.broadcasted_iota(jnp.int32, q.shape, 1) < HEAD_DIM
    zero = jnp.zeros_like(q)
    q_s[0] = jnp.where(lo, q, zero)
    q_s[1] = jnp.where(lo, zero, q)


def _sb_prompt_kernel(q_ref, kt_ref, v_ref, su_ref, o_ref, q_s, za_s, zb_s, r_s, acc_s, *, tq, tk):
    qi = pl.program_id(2)
    _split_heads(q_ref, q_s)
    r_s[...] = jnp.zeros(r_s.shape, F32)
    acc_s[...] = jnp.zeros(acc_s.shape, F32)

    def logits(j, z_ref):
        start = pl.multiple_of(jnp.maximum(j, 0) * tk, tk)
        kt = kt_ref[:, pl.ds(start, tk)]
        for h in range(2):
            z_ref[h] = _dot(q_s[h], kt)

    def update(j, z_ref, masked):
        start = pl.multiple_of(j * tk, tk)
        v = v_ref[pl.ds(start, tk), :]
        su = su_ref[...]
        if masked:
            row = lax.broadcasted_iota(jnp.int32, (tq, tk), 0) + qi * tq
            col = lax.broadcasted_iota(jnp.int32, (tq, tk), 1) + start
            strict = col < row
        exponents = []
        for h in range(2):
            z = z_ref[h]
            neg_abs = pltpu.bitcast(pltpu.bitcast(z, jnp.int32) | SIGN_BIT, F32)
            ls = jnp.minimum(z, 0.0) - jnp.log2(1.0 + jnp.exp2(neg_abs))
            lk = ls - z
            if masked:
                lk = jnp.where(strict, lk, 0.0)
            r_old = r_s[h]
            after = _dot(lk.astype(BF16), su) + jnp.concatenate([r_old] * (tk // LANES), axis=1)
            exponents.append(ls + after)
            r_s[h] = r_old + jnp.sum(lk, axis=-1, keepdims=True)
        for h in range(2):
            a = jnp.exp2(exponents[h])
            if masked:
                a = jnp.where(strict, a, 0.0)
            acc_s[h] = acc_s[h] + _dot(a.astype(BF16), v)

    assert tq == 2 * tk
    top = 2 * qi + 1
    logits(top, za_s)
    logits(top - 1, zb_s)
    update(top, za_s, True)
    logits(top - 2, za_s)
    update(top - 1, zb_s, True)

    def some_weight_nonzero():
        r_max = jnp.max(jnp.maximum(r_s[0], r_s[1]), axis=0, keepdims=True)
        return r_max[0, 0] > UNDERFLOW_LOG2

    def pair(j):
        logits(j - 1, zb_s)
        update(j, za_s, False)

        @pl.when(some_weight_nonzero())
        def _():
            logits(j - 2, za_s)
            update(j - 1, zb_s, False)

    def more(c):
        i, alive = c
        return jnp.logical_and(i < qi, alive)

    def step(c):
        i, _ = c
        pair(top - 2 - 2 * i)
        return i + 1, some_weight_nonzero()

    lax.while_loop(more, step, (jnp.int32(0), some_weight_nonzero()))

    lo = lax.broadcasted_iota(jnp.int32, (tq, LANES), 1) < HEAD_DIM
    o_ref[...] = jnp.where(lo, acc_s[0], acc_s[1])


def _fox_prompt(qt, kaug, vt, *, seq, tq, tk):
    t = kaug.shape[0]
    n_seq, nq = t // seq, seq // tq
    o_spec = pl.BlockSpec((tq, LANES), lambda b, p, i: (b * nq + i, p))
    return pl.pallas_call(
        functools.partial(_fox_prompt_kernel, tq=tq, tk=tk), grid=(n_seq, N_PAIRS, nq),
        in_specs=[pl.BlockSpec((2 * LANES, tq), lambda b, p, i: (p, b * nq + i)),
                  pl.BlockSpec((seq, 2 * LANES), lambda b, p, i: (b, p)),
                  pl.BlockSpec((None, 2 * VT_ROWS, seq), lambda b, p, i: (b, p, 0))],
        out_specs=o_spec, out_shape=jax.ShapeDtypeStruct((t, D_ATT), F32),
        scratch_shapes=[pltpu.VMEM((2, tk, tq), F32), pltpu.VMEM((2, tk, tq), F32),
                        pltpu.VMEM((2, 1, tq), F32), pltpu.VMEM((2, VT_ROWS, tq), F32)],
        compiler_params=_params("arbitrary", "arbitrary", "arbitrary"), name="fox_prompt",
    )(qt, kaug, vt)


def _sb_prompt(q, kt, v, su, *, seq, tq, tk):
    t = q.shape[0]
    n_seq, nq = t // seq, seq // tq
    q_spec = pl.BlockSpec((tq, LANES), lambda b, p, i: (b * nq + i, p))
    return pl.pallas_call(
        functools.partial(_sb_prompt_kernel, tq=tq, tk=tk), grid=(n_seq, N_PAIRS, nq),
        in_specs=[q_spec,
                  pl.BlockSpec((None, LANES, seq), lambda b, p, i: (b, p, 0)),
                  pl.BlockSpec((seq, LANES), lambda b, p, i: (b, p)),
                  pl.BlockSpec((tk, tk), lambda b, p, i: (0, 0))],
        out_specs=q_spec, out_shape=jax.ShapeDtypeStruct((t, D_ATT), F32),
        scratch_shapes=[pltpu.VMEM((2, tq, LANES), BF16), pltpu.VMEM((2, tq, tk), F32),
                        pltpu.VMEM((2, tq, tk), F32), pltpu.VMEM((2, tq, LANES), F32),
                        pltpu.VMEM((2, tq, LANES), F32)],
        compiler_params=_params("arbitrary", "arbitrary", "arbitrary"), name="sb_prompt",
    )(q, kt, v, su)


def _pick_rows(parts):
    rows = lax.broadcasted_iota(jnp.int32, parts[0].shape, 0)
    out = jnp.zeros_like(parts[0])
    for h, part in enumerate(parts):
        out = jnp.where(rows == h, part, out)
    return out


def _suffix_over_pages(x, su, carry):
    g_n = x.shape[1] // PAGE_SIZE
    stacked = jnp.concatenate([x[:, g * PAGE_SIZE:(g + 1) * PAGE_SIZE] for g in range(g_n)], axis=0)
    hi, mid, lo = _split3(stacked)
    inside = _dot(hi, su) + _dot(mid, su) + _dot(lo, su)
    total = jnp.sum(stacked, axis=-1, keepdims=True)
    outs = []
    for g in range(g_n):
        outs.append(inside[g * HEAD_ROWS:(g + 1) * HEAD_ROWS] + carry)
        carry = carry + total[g * HEAD_ROWS:(g + 1) * HEAD_ROWS]
    return jnp.concatenate(outs, axis=1), carry


def _wide(page_refs, h):
    return jnp.concatenate([r[h].astype(BF16) for r in page_refs], axis=1)


def _page_scores(q, page_refs):
    qb = q.astype(BF16)
    return _pick_rows([_dot(qb, _wide(page_refs, h)) for h in range(N_HEADS)])


def _decode_fox_kernel(pt_ref, qf_ref, knf_ref, vnf_ref, lfn_ref, su_ref, *refs):
    del pt_ref
    g_n = PAGES_PER_STEP
    kf, vf, lf = (refs[k * g_n:(k + 1) * g_n] for k in range(3))
    of_ref, m_s, l_s, g_s, acc_s = refs[3 * g_n:]
    p = pl.program_id(1)

    @pl.when(p == 0)
    def _():
        m_s[...] = jnp.full(m_s.shape, NEG, F32)
        l_s[...] = jnp.zeros(l_s.shape, F32)
        g_s[...] = jnp.zeros(g_s.shape, F32)
        acc_s[...] = jnp.zeros(acc_s.shape, F32)

    qf = qf_ref[...]
    zeros = jnp.zeros((HEAD_ROWS - 8, PAGE_SIZE), F32)
    lf_w = jnp.concatenate([jnp.concatenate([r[...], zeros], axis=0) for r in lf], axis=1)
    decay, g_new = _suffix_over_pages(lf_w, su_ref[...], g_s[...])
    g_s[...] = g_new
    s = _page_scores(qf, kf) + decay + lfn_ref[...]
    m_old = m_s[...]
    m_new = jnp.maximum(m_old, jnp.max(s, axis=-1, keepdims=True))
    alpha = jnp.exp(m_old - m_new)
    pr = jnp.exp(s - m_new)
    l_s[...] = alpha * l_s[...] + jnp.sum(pr, axis=-1, keepdims=True)
    m_s[...] = m_new
    prb = pr.astype(BF16)
    for h in range(N_HEADS):
        acc_s[h] = alpha * acc_s[h] + _dot_nt(prb, _wide(vf, h))

    @pl.when(p == pl.num_programs(1) - 1)
    def _():
        s_self = jnp.sum(qf * knf_ref[...], axis=-1, keepdims=True)
        m_old = m_s[...]
        m_new = jnp.maximum(m_old, s_self)
        alpha = jnp.exp(m_old - m_new)
        p_self = jnp.exp(s_self - m_new)
        inv_l = 1.0 / (alpha * l_s[...] + p_self)
        own = p_self * vnf_ref[...]
        of_ref[...] = _pick_rows([alpha * acc_s[h] + own for h in range(N_HEADS)]) * inv_l


def _decode_sb_kernel(pt_ref, qs_ref, su_ref, *refs, g_n):
    del pt_ref
    ks, vs = refs[:g_n], refs[g_n:2 * g_n]
    os_ref, carry_ref, r_s, acc_s = refs[2 * g_n:]
    p = pl.program_id(1)

    @pl.when(p == 0)
    def _():
        r_s[...] = jnp.zeros(r_s.shape, F32)
        acc_s[...] = jnp.zeros(acc_s.shape, F32)

    @pl.when(jnp.max(r_s[...], axis=0, keepdims=True)[0, 0] > UNDERFLOW_LN)
    def _():
        z = _page_scores(qs_ref[...], ks)
        lk = -_softplus(z)
        after, r_new = _suffix_over_pages(lk, su_ref[...], r_s[...])
        r_s[...] = r_new
        ab = jnp.exp(z + lk + after).astype(BF16)
        for h in range(N_HEADS):
            acc_s[h] = acc_s[h] + _dot_nt(ab, _wide(vs, h))

    @pl.when(p == pl.num_programs(1) - 1)
    def _():
        os_ref[...] = _pick_rows([acc_s[h] for h in range(N_HEADS)])
        carry_ref[...] = r_s[...]


def _decode_specs(layer, page_table, g_n):
    n_b, n_pages = page_table.shape
    per_b = lambda shape: pl.BlockSpec((None,) + shape, lambda b, p, pt: (b,) + (0,) * len(shape))

    def pages(shape):
        return [pl.BlockSpec((None, None) + shape,
                             lambda b, p, pt, g=g: (layer, pt[b, n_pages - 1 - (p * g_n + g)])
                             + (0,) * len(shape)) for g in range(g_n)]

    su_spec = pl.BlockSpec((PAGE_SIZE, PAGE_SIZE), lambda b, p, pt: (0, 0))
    return per_b, pages, su_spec


def _decode_fox(layer, page_table, qf, knf, vnf, lfn, su, cache_kf, cache_vf, cache_lf):
    n_b, n_pages = page_table.shape
    g_n = PAGES_PER_STEP
    per_b, pages, su_spec = _decode_specs(layer, page_table, g_n)
    head_blk = per_b((HEAD_ROWS, HEAD_DIM))
    kv_pages = pages((N_HEADS, HEAD_DIM, PAGE_SIZE))
    col = pltpu.VMEM((HEAD_ROWS, 1), F32)
    return pl.pallas_call(
        _decode_fox_kernel,
        grid_spec=pltpu.PrefetchScalarGridSpec(
            num_scalar_prefetch=1, grid=(n_b, n_pages // g_n),
            in_specs=[head_blk, head_blk, head_blk, per_b((HEAD_ROWS, 1)), su_spec]
            + kv_pages + kv_pages + pages((8, PAGE_SIZE)),
            out_specs=head_blk,
            scratch_shapes=[col, col, col, pltpu.VMEM((N_HEADS, HEAD_ROWS, HEAD_DIM), F32)]),
        out_shape=jax.ShapeDtypeStruct((n_b, HEAD_ROWS, HEAD_DIM), F32),
        compiler_params=_params("arbitrary", "arbitrary"), name="decode_fox",
    )(page_table, qf, knf, vnf, lfn, su, *([cache_kf] * g_n), *([cache_vf] * g_n),
      *([cache_lf] * g_n))


def _decode_sb(layer, page_table, qs, su, cache_ks, cache_vs, *, n_steps, g_n):
    n_b = page_table.shape[0]
    per_b, pages, su_spec = _decode_specs(layer, page_table, g_n)
    head_blk = per_b((HEAD_ROWS, HEAD_DIM))
    col_blk = per_b((HEAD_ROWS, 1))
    kv_pages = pages((N_HEADS, HEAD_DIM, PAGE_SIZE))
    return pl.pallas_call(
        functools.partial(_decode_sb_kernel, g_n=g_n),
        grid_spec=pltpu.PrefetchScalarGridSpec(
            num_scalar_prefetch=1, grid=(n_b, n_steps),
            in_specs=[head_blk, su_spec] + kv_pages + kv_pages,
            out_specs=[head_blk, col_blk],
            scratch_shapes=[pltpu.VMEM((HEAD_ROWS, 1), F32),
                            pltpu.VMEM((N_HEADS, HEAD_ROWS, HEAD_DIM), F32)]),
        out_shape=[jax.ShapeDtypeStruct((n_b, HEAD_ROWS, HEAD_DIM), F32),
                   jax.ShapeDtypeStruct((n_b, HEAD_ROWS, 1), F32)],
        compiler_params=_params("arbitrary", "arbitrary"), name="decode_sb",
    )(page_table, qs, su, *([cache_ks] * g_n), *([cache_vs] * g_n))


def _decode_sb_all(layer, page_table, qs, su, cache_ks, cache_vs):
    n_pages = page_table.shape[1]
    g_first = min(SB_FIRST_PAGES, n_pages)
    g_rest = min(PAGES_PER_STEP, n_pages)
    o_new, carry = _decode_sb(layer, page_table, qs, su, cache_ks, cache_vs, n_steps=1, g_n=g_first)
    if g_first == n_pages:
        return o_new
    return lax.cond(
        jnp.max(carry) > UNDERFLOW_LN,
        lambda: _decode_sb(layer, page_table, qs, su, cache_ks, cache_vs,
                           n_steps=n_pages // g_rest, g_n=g_rest)[0],
        lambda: o_new)


def _outproj_kernel(of_ref, oc_ref, os_ref, x_ref, gf_ref, gs_ref, w_ref, o_ref):
    nf = (_rms(of_ref[...]) * gf_ref[...]).astype(BF16)
    ns = (_rms(os_ref[...]) * gs_ref[...]).astype(BF16)
    y = (_dot(nf, w_ref[0:D_ATT, :]) + _dot(oc_ref[...], w_ref[D_ATT:D_ATT + D_CONV, :])
         + _dot(ns, w_ref[D_ATT + D_CONV:, :]))
    o_ref[...] = x_ref[...] + y


def _outproj(of, oc, os_, x, gf, gs, w, *, tm):
    t = x.shape[0]
    row = lambda width: pl.BlockSpec((tm, width), lambda i: (i, 0))
    return pl.pallas_call(
        _outproj_kernel, grid=(t // tm,),
        in_specs=[row(D_ATT), row(D_CONV), row(D_ATT), row(D_MODEL), _full((1, D_ATT)),
                  _full((1, D_ATT)), _full((D_MODEL, D_MODEL))],
        out_specs=row(D_MODEL), out_shape=jax.ShapeDtypeStruct((t, D_MODEL), F32),
        compiler_params=_params("arbitrary"), name="outproj",
    )(of, oc, os_, x, gf, gs, w)


def _route(logits):
    lane_i = lax.broadcasted_iota(jnp.int32, logits.shape, 1)
    lane = lane_i.astype(F32)
    lane_group = jnp.right_shift(lane_i, EXPERTS_PER_GROUP.bit_length() - 1).astype(F32)
    big = float(LANES)
    is_group = (lane_i >= N_EXPERTS) & (lane_i < N_EXPERTS + N_GROUPS)
    gl = jnp.where(is_group, logits, NEG)
    g_max = jnp.max(gl, axis=-1, keepdims=True)
    g_idx = jnp.min(jnp.where(gl == g_max, lane, big), axis=-1, keepdims=True) - N_EXPERTS
    p_group = 1.0 / jnp.sum(jnp.exp(gl - g_max), axis=-1, keepdims=True)
    in_group = (lane_i < N_EXPERTS) & (lane_group == g_idx)
    el = jnp.where(in_group, logits, NEG)
    m1 = jnp.max(el, axis=-1, keepdims=True)
    i1 = jnp.min(jnp.where(el == m1, lane, big), axis=-1, keepdims=True)
    el2 = jnp.where(lane == i1, NEG, el)
    m2 = jnp.max(el2, axis=-1, keepdims=True)
    i2 = jnp.min(jnp.where(el2 == m2, lane, big), axis=-1, keepdims=True)
    e21 = jnp.exp(m2 - m1)
    w1 = p_group / (1.0 + e21)
    return jnp.where(lane == i1, w1, 0.0) + jnp.where(lane == i2, w1 * e21, 0.0)


def _moe_kernel(x_ref, gn_ref, wr_ref, br_ref, wg_ref, wu_ref, wd_ref, gfin_ref, o_ref,
                xn_s, gate_s, *, final):
    g = pl.program_id(1)

    @pl.when(g == 0)
    def _():
        x = x_ref[...]
        xn = _rms(x) * gn_ref[...]
        xn_s[...] = xn.astype(BF16)
        logits = jnp.dot(xn, wr_ref[...], preferred_element_type=F32,
                         precision=lax.Precision.HIGHEST) + br_ref[...]
        gate = _route(logits)
        for gi in range(N_GROUPS):
            shift = (LANES - gi * EXPERTS_PER_GROUP) % LANES
            gate_s[gi] = pltpu.roll(gate, shift, 1) if shift else gate
        o_ref[...] = x

    xb = xn_s[...]
    hg = _dot(xb, wg_ref[...])
    hu = _dot(xb, wu_ref[...])
    h = hg * (1.0 / (1.0 + jnp.exp(-hg))) * hu
    gate = gate_s[g]
    hgated = jnp.concatenate(
        [h[:, e * D_EXPERT:(e + 1) * D_EXPERT] * gate[:, e:e + 1] for e in range(EXPERTS_PER_GROUP)],
        axis=1).astype(BF16)
    o_ref[...] += _dot(hgated, wd_ref[...])

    if final:
        @pl.when(g == N_GROUPS - 1)
        def _():
            o_ref[...] = _rms(o_ref[...]) * gfin_ref[...]


def _moe(x, gn, wr, br, wg, wu, wd, gfin, *, tm, final):
    t = x.shape[0]
    row = pl.BlockSpec((tm, D_MODEL), lambda i, g: (i, 0))
    const = lambda shape: pl.BlockSpec(shape, lambda i, g: (0,) * len(shape))
    return pl.pallas_call(
        functools.partial(_moe_kernel, final=final), grid=(t // tm, N_GROUPS),
        in_specs=[row, const((1, D_MODEL)), const((D_MODEL, LANES)), const((1, LANES)),
                  pl.BlockSpec((D_MODEL, D_GROUP), lambda i, g: (0, g)),
                  pl.BlockSpec((D_MODEL, D_GROUP), lambda i, g: (0, g)),
                  pl.BlockSpec((D_GROUP, D_MODEL), lambda i, g: (g, 0)),
                  const((1, D_MODEL))],
        out_specs=row, out_shape=jax.ShapeDtypeStruct((t, D_MODEL), F32),
        scratch_shapes=[pltpu.VMEM((tm, D_MODEL), BF16), pltpu.VMEM((N_GROUPS, tm, LANES), F32)],
        compiler_params=_params("arbitrary", "arbitrary"), name="moe",
    )(x, gn, wr, br, wg, wu, wd, gfin)


def _tri(n, rel):
    r = lax.broadcasted_iota(jnp.int32, (n, n), 0)
    c = lax.broadcasted_iota(jnp.int32, (n, n), 1)
    return rel(r, c).astype(BF16)


def kernel(x_prompt, x_sample, cache_k_fox, cache_v_fox, cache_logf_fox, cache_k_sb, cache_v_sb,
           state_conv, page_table, g_mix_norm, w_in, b_f, conv_w, g_mix_out, w_out, g_ffn_norm,
           w_router_g, b_router_g, w_router_e, b_router_e, w_gate, w_up, w_down, g_final):
    depth = w_in.shape[0]
    n_seq, seq, _ = x_prompt.shape
    n_dec = x_sample.shape[0]
    t_p = n_seq * seq
    tm_in, tm_out, tm_moe = 256, 512, 512
    tq_f, tk_f, tq_s, tk_s = 512, 256, 512, 256

    c_fl = 3 * D_ATT
    c_end = c_fl + N_HEADS
    wq_f, wk_f, wv_f = w_in[..., :D_ATT], w_in[..., D_ATT:2 * D_ATT], w_in[..., 2 * D_ATT:c_fl]
    w_f = w_in[..., c_fl:c_end]
    w_cv = w_in[..., c_end:c_end + 3 * D_CONV]
    c_s = c_end + 3 * D_CONV
    wq_s, wk_s, wv_s = (w_in[..., c_s:c_s + D_ATT], w_in[..., c_s + D_ATT:c_s + 2 * D_ATT],
                        w_in[..., c_s + 2 * D_ATT:])
    pad_cols = lambda w, n: jnp.pad(w, ((0, 0), (0, 0), (0, n - w.shape[-1])))
    w_in_r = jnp.concatenate([wq_f, wk_f, wv_f, w_cv, wq_s, wk_s, wv_s, pad_cols(w_f, LANES)],
                             axis=-1).astype(BF16)
    wk_slots = jnp.pad(wk_f.reshape(depth, D_MODEL, N_HEADS, HEAD_DIM),
                       ((0, 0), (0, 0), (0, 0), (0, LANES - HEAD_DIM))).reshape(depth, D_MODEL, D_SLOT)
    w_nn = jnp.concatenate([wk_slots, w_cv, wq_s, wv_s, pad_cols(w_f, LANES)], axis=-1).astype(BF16)
    w_nt = jnp.swapaxes(jnp.concatenate([wq_f, wk_f, wv_f, wk_s, wv_s, pad_cols(w_f, HEAD_ROWS)],
                                        axis=-1), 1, 2).astype(BF16)
    bf_r = jnp.pad(b_f, ((0, 0), (0, LANES - N_HEADS)))[:, None, :]
    bft_r = jnp.pad(b_f, ((0, 0), (0, HEAD_ROWS - N_HEADS)))[:, :, None]
    cw_r = jnp.pad(conv_w, ((0, 0), (0, 8 - CONV_WIDTH), (0, 0)))
    g_f = g_mix_out[:, None, :D_ATT]
    g_c = g_mix_out[:, None, D_ATT:D_ATT + D_CONV]
    g_s = g_mix_out[:, None, D_ATT + D_CONV:]
    w_out_r = w_out.astype(BF16)
    w_r = jnp.concatenate(
        [w_router_e, w_router_g,
         jnp.zeros((depth, D_MODEL, LANES - N_EXPERTS - N_GROUPS), F32)], axis=-1)
    b_r = jnp.concatenate(
        [b_router_e, b_router_g, jnp.zeros((depth, LANES - N_EXPERTS - N_GROUPS), F32)],
        axis=-1)[:, None, :]
    w_gate_r = jnp.transpose(w_gate, (0, 2, 1, 3)).reshape(depth, D_MODEL, -1).astype(BF16)
    w_up_r = jnp.transpose(w_up, (0, 2, 1, 3)).reshape(depth, D_MODEL, -1).astype(BF16)
    w_down_r = w_down.reshape(depth, -1, D_MODEL).astype(BF16)
    g_mix = g_mix_norm[:, None, :]
    g_ffn = g_ffn_norm[:, None, :]
    g_fin = g_final[None, :]

    tri_in = _tri(tm_in, lambda r, c: c <= r)
    su_sb = _tri(tk_s, lambda r, c: r > c)
    su_page = _tri(PAGE_SIZE, lambda r, c: r > c)
    pr = lax.broadcasted_iota(jnp.int32, (3, LANES, D_SLOT), 1)
    pc = lax.broadcasted_iota(jnp.int32, (3, LANES, D_SLOT), 2)
    pi = lax.broadcasted_iota(jnp.int32, (3, LANES, D_SLOT), 0)
    place = ((pr < N_HEADS) & (pc == pr * LANES + HEAD_DIM + pi)).astype(BF16)

    to_lanes = lambda c: jnp.transpose(c, (0, 1, 3, 4, 2))
    ck_f, cv_f, ck_s, cv_s = map(to_lanes, (cache_k_fox, cache_v_fox, cache_k_sb, cache_v_sb))
    c_lf = jnp.pad(jnp.swapaxes(cache_logf_fox, 2, 3), ((0, 0), (0, 0), (0, 8 - N_HEADS), (0, 0)))

    def heads16(a):
        return jnp.pad(a.reshape(n_dec, N_HEADS, HEAD_DIM), ((0, 0), (0, HEAD_ROWS - N_HEADS), (0, 0)))

    xp = x_prompt.reshape(t_p, D_MODEL)
    xs = x_sample.reshape(n_dec, D_MODEL)
    stacked = None
    conv_p = []
    st_s = [[] for _ in range(6)]
    for l in range(depth):
        final = l == depth - 1
        outs = _inproj_prompt(l, depth, xp, g_mix[l], w_nn[l], w_nt[l], bf_r[l], bft_r[l], cw_r[l],
                              g_c[l], tri_in, place, stacked, seq=seq, tm=tm_in)
        qtf, kaug, vtf16, qs, kts16, vs16, oc, utail = outs[:8]
        stacked = outs[8:]
        of = _fox_prompt(qtf, kaug, vtf16, seq=seq, tq=tq_f, tk=tk_f)
        os_ = _sb_prompt(qs, kts16, vs16, su_sb, seq=seq, tq=tq_s, tk=tk_s)
        xp = _outproj(of, oc, os_, xp, g_f[l], g_s[l], w_out_r[l], tm=tm_out)
        xp = _moe(xp, g_ffn[l], w_r[l], b_r[l], w_gate_r[l], w_up_r[l], w_down_r[l], g_fin,
                  tm=tm_moe, final=final)
        conv_p.append(utail.reshape(n_seq, 8, D_CONV)[:, 8 - (CONV_WIDTH - 1):])
        prev = state_conv[l]
        (qf, kf32, vf32, qs, ks32, vs32, logf, oc, u) = _inproj_sample(
            xs, g_mix[l], w_in_r[l], bf_r[l], cw_r[l], g_c[l], prev[:, 0], prev[:, 1])
        lfn = jnp.pad(logf[:, :N_HEADS], ((0, 0), (0, HEAD_ROWS - N_HEADS)))[:, :, None]
        of = _decode_fox(l, page_table, heads16(qf), heads16(kf32), heads16(vf32), lfn, su_page,
                         ck_f, cv_f, c_lf)
        os_ = _decode_sb_all(l, page_table, heads16(qs), su_page, ck_s, cv_s)
        of = of[:, :N_HEADS].reshape(n_dec, D_ATT)
        os_ = os_[:, :N_HEADS].reshape(n_dec, D_ATT)
        xs = _outproj(of, oc, os_, xs, g_f[l], g_s[l], w_out_r[l], tm=n_dec)
        xs = _moe(xs, g_ffn[l], w_r[l], b_r[l], w_gate_r[l], w_up_r[l], w_down_r[l], g_fin,
                  tm=n_dec, final=final)
        hd1 = lambda a: a.reshape(n_dec, 1, N_HEADS, HEAD_DIM)
        new_s = (hd1(kf32), hd1(vf32), logf[:, :N_HEADS].reshape(n_dec, 1, N_HEADS), hd1(ks32), hd1(vs32),
                 jnp.stack([prev[:, 1], u], axis=1))
        for acc, a in zip(st_s, new_s):
            acc.append(a)

    kt_f, vt_f, kt_s, vt_s, lf_t = stacked
    from_lanes = lambda a: jnp.transpose(a.reshape(depth, n_seq, N_HEADS, HEAD_DIM, seq), (0, 1, 4, 2, 3))
    logf_p = jnp.transpose(lf_t[:, :N_HEADS].reshape(depth, N_HEADS, n_seq, seq), (0, 2, 3, 1))
    y_prompt = xp.reshape(n_seq, seq, D_MODEL)
    y_sample = xs.reshape(n_dec, 1, D_MODEL)
    return ((y_prompt, y_sample, from_lanes(kt_f), from_lanes(vt_f), logf_p, from_lanes(kt_s),
             from_lanes(vt_s), jnp.stack(conv_p)) + tuple(jnp.stack(a) for a in st_s))
```
